```python
import jax, jax.numpy as jnp
from jax import lax
import numpy as np

D_MODEL = 4096
BATCH = 2
SEQ = 8192
DEPTH = 1

CHUNK = 64
PLE_DIM = 256
D_RNN = 2048
RNN_HEADS = 16
RNN_HEAD_DIM = D_RNN // RNN_HEADS
CONV_WIDTH = 4
LRU_C = 8.0
D_POOL = 2048
POOL_WINDOWS = (2, 4, 8, 16)
POOL_MAX = max(POOL_WINDOWS)
POOL_GROUPS = len(POOL_WINDOWS)
POOL_GROUP = D_POOL // POOL_GROUPS
D_MIX = D_RNN + D_POOL
D_IN = 2 * D_RNN + D_POOL
D_FF = 4 * D_MODEL
EPS = 1e-6

kernel_name = "hybrid_rglru_multiscale_pool_block"


def rmsnorm(x, g):
    xf = x.astype(jnp.float32)
    y = xf * lax.rsqrt(jnp.mean(xf * xf, axis=-1, keepdims=True) + EPS)
    return (y * g.astype(jnp.float32)).astype(x.dtype)


def causal_depthwise_conv(x, w, b):
    y = lax.conv_general_dilated(
        x, w[:, None, :], window_strides=(1,), padding=[(CONV_WIDTH - 1, 0)],
        dimension_numbers=("NWC", "WIO", "NWC"), feature_group_count=x.shape[-1])
    return y + b


def block_diag_linear(x, w, b):
    bsz, s, _ = x.shape
    xh = x.reshape(bsz, s, RNN_HEADS, RNN_HEAD_DIM)
    y = jnp.einsum("bshi,hij->bshj", xh, w).reshape(bsz, s, D_RNN)
    return y + b


def _lin_combine(left, right):
    a_l, b_l = left
    a_r, b_r = right
    return a_l * a_r, a_r * b_l + b_r


def chunked_linear_scan(a, b):
    bsz, s, c = a.shape
    n_chunks = s // CHUNK
    a_c = a.reshape(bsz, n_chunks, CHUNK, c).transpose(1, 0, 2, 3)
    b_c = b.reshape(bsz, n_chunks, CHUNK, c).transpose(1, 0, 2, 3)

    def step(h_prev, ab):
        ac, bc = ab
        a_cum, h_loc = lax.associative_scan(_lin_combine, (ac, bc), axis=1)
        h = h_loc + a_cum * h_prev[:, None, :]
        return h[:, -1], h

    _, hs = lax.scan(step, jnp.zeros((bsz, c), jnp.float32), (a_c, b_c))
    return hs.transpose(1, 0, 2, 3).reshape(bsz, s, c)


def rg_lru(x, w_a, b_a, w_x, b_x, lam):
    r = jax.nn.sigmoid(block_diag_linear(x, w_a, b_a).astype(jnp.float32))
    i = jax.nn.sigmoid(block_diag_linear(x, w_x, b_x).astype(jnp.float32))
    log_a = -LRU_C * r * jax.nn.softplus(-lam.astype(jnp.float32))
    a = jnp.exp(log_a)
    mult = jnp.sqrt(-jnp.expm1(2.0 * log_a))
    h = chunked_linear_scan(a, mult * i * x.astype(jnp.float32))
    return h.astype(x.dtype)


def multiscale_pool(v, w_pool, b_pool):
    bsz, s, _ = v.shape
    vf = v.astype(jnp.float32)
    cs = jnp.pad(jnp.cumsum(vf, axis=1), ((0, 0), (POOL_MAX, 0), (0, 0)))
    pos = jnp.arange(1, s + 1, dtype=jnp.int32)
    outs = []
    for g, w in enumerate(POOL_WINDOWS):
        lo, hi = g * POOL_GROUP, (g + 1) * POOL_GROUP
        win_sum = cs[:, POOL_MAX:POOL_MAX + s, lo:hi] - cs[:, POOL_MAX - w:POOL_MAX - w + s, lo:hi]
        cnt = jnp.minimum(pos, w).astype(jnp.float32)[None, :, None]
        outs.append(win_sum / cnt)
    z = (jnp.concatenate(outs, axis=-1) - vf).astype(v.dtype)
    z = z.reshape(bsz, s, POOL_GROUPS, POOL_GROUP)
    y = jnp.einsum("bsgi,gij->bsgj", z, w_pool).reshape(bsz, s, D_POOL)
    return y + b_pool


def setup_inputs(seed: int = 0) -> dict:
    key = jax.random.key(seed)
    ks = jax.random.split(key, 24)
    f32 = jnp.float32

    def nrm(k, shape, fan_in):
        return jax.random.normal(k, shape, f32) * (fan_in ** -0.5)

    def gain(k, shape):
        return 1.0 + 0.01 * jax.random.normal(k, shape, f32)

    def bias(k, shape):
        return 0.01 * jax.random.normal(k, shape, f32)

    u = jax.random.uniform(ks[10], (DEPTH, D_RNN), f32, minval=0.9, maxval=0.999)
    a0 = u ** (1.0 / LRU_C)
    lru_lambda = jnp.log(a0) - jnp.log1p(-a0)
    return {
        "x": jax.random.normal(ks[0], (BATCH, SEQ, D_MODEL), f32),
        "p": jax.random.normal(ks[1], (DEPTH, BATCH, SEQ, PLE_DIM), f32),
        "norm_mix_g": gain(ks[2], (DEPTH, D_MODEL)),
        "w_in": nrm(ks[3], (DEPTH, D_MODEL, D_IN), D_MODEL),
        "conv_w": nrm(ks[4], (DEPTH, CONV_WIDTH, D_RNN), CONV_WIDTH),
        "conv_b": bias(ks[5], (DEPTH, D_RNN)),
        "w_rg_a": nrm(ks[6], (DEPTH, RNN_HEADS, RNN_HEAD_DIM, RNN_HEAD_DIM), RNN_HEAD_DIM),
        "b_rg_a": bias(ks[7], (DEPTH, D_RNN)),
        "w_rg_x": nrm(ks[8], (DEPTH, RNN_HEADS, RNN_HEAD_DIM, RNN_HEAD_DIM), RNN_HEAD_DIM),
        "b_rg_x": bias(ks[9], (DEPTH, D_RNN)),
        "lru_lambda": lru_lambda,
        "beta_rnn": gain(ks[11], (DEPTH, D_RNN)),
        "w_pool": nrm(ks[12], (DEPTH, POOL_GROUPS, POOL_GROUP, POOL_GROUP), POOL_GROUP),
        "b_pool": bias(ks[13], (DEPTH, D_POOL)),
        "pool_scale": gain(ks[14], (DEPTH, D_POOL)),
        "w_out": nrm(ks[15], (DEPTH, D_MIX, D_MODEL), D_MIX),
        "norm_mlp_g": gain(ks[16], (DEPTH, D_MODEL)),
        "w_up": nrm(ks[17], (DEPTH, D_MODEL, D_FF), D_MODEL),
        "w_down": nrm(ks[18], (DEPTH, D_FF, D_MODEL), D_FF),
        "norm_ple_g": gain(ks[19], (DEPTH, D_MODEL)),
        "w_ple_gate": nrm(ks[20], (DEPTH, D_MODEL, D_MODEL), D_MODEL),
        "w_ple_proj": nrm(ks[21], (DEPTH, PLE_DIM, D_MODEL), PLE_DIM),
        "final_norm_g": gain(ks[22], (D_MODEL,)),
    }


def reference(x, p, norm_mix_g, w_in, conv_w, conv_b, w_rg_a, b_rg_a, w_rg_x, b_rg_x,
              lru_lambda, beta_rnn, w_pool, b_pool, pool_scale, w_out, norm_mlp_g,
              w_up, w_down, norm_ple_g, w_ple_gate, w_ple_proj, final_norm_g):
    h = x
    for i in range(DEPTH):
        u = rmsnorm(h, norm_mix_g[i])
        proj = u @ w_in[i]
        xr = proj[..., :D_RNN]
        gr = proj[..., D_RNN:2 * D_RNN]
        v = proj[..., 2 * D_RNN:]
        xr = causal_depthwise_conv(xr, conv_w[i], conv_b[i])
        y_rnn = rg_lru(xr, w_rg_a[i], b_rg_a[i], w_rg_x[i], b_rg_x[i], lru_lambda[i])
        y_rnn = rmsnorm(y_rnn * jax.nn.gelu(gr, approximate=True), beta_rnn[i])
        y_pool = rmsnorm(multiscale_pool(v, w_pool[i], b_pool[i]), pool_scale[i])
        h = h + jnp.concatenate([y_rnn, y_pool], axis=-1) @ w_out[i]
        u = rmsnorm(h, norm_mlp_g[i])
        h = h + jnp.square(jax.nn.relu(u @ w_up[i])) @ w_down[i]
        gate = jax.nn.sigmoid(rmsnorm(h, norm_ple_g[i]) @ w_ple_gate[i])
        h = h + gate * (p[i] @ w_ple_proj[i])
    return rmsnorm(h, final_norm_g)
```

```python
import functools

import jax
import jax.numpy as jnp
from jax import lax
from jax.experimental import pallas as pl
from jax.experimental.pallas import tpu as pltpu

F32 = jnp.float32
BF16 = jnp.bfloat16

EPS = 1e-6
LRU_C = 8.0
CONV_WIDTH = 4
POOL_WINDOWS = (2, 4, 8, 16)
LANES = 128
SUBLANES = 8
VMEM_LIMIT = 60 * 1024 * 1024


def _rms_scale(x, g):
    ms = jnp.mean(x * x, axis=-1, keepdims=True)
    return x * lax.rsqrt(ms + EPS) * g


def _params(n_axes):
    return pltpu.CompilerParams(
        dimension_semantics=("arbitrary",) * n_axes, vmem_limit_bytes=VMEM_LIMIT)


def _norm_proj_kernel(x_ref, g_ref, w_ref, o_ref, u_ref):
    @pl.when(pl.program_id(1) == 0)
    def _():
        u_ref[...] = _rms_scale(x_ref[...], g_ref[...]).astype(BF16)

    o_ref[...] = jnp.dot(u_ref[...], w_ref[...], preferred_element_type=F32)


def _norm_proj(x, g, w, *, bm, bn):
    m, d = x.shape
    n = w.shape[1]
    return pl.pallas_call(
        _norm_proj_kernel,
        grid=(m // bm, n // bn),
        in_specs=[
            pl.BlockSpec((bm, d), lambda i, j: (i, 0)),
            pl.BlockSpec((1, d), lambda i, j: (0, 0)),
            pl.BlockSpec((d, bn), lambda i, j: (0, j)),
        ],
        out_specs=pl.BlockSpec((bm, bn), lambda i, j: (i, j)),
        out_shape=jax.ShapeDtypeStruct((m, n), F32),
        scratch_shapes=[pltpu.VMEM((bm, d), BF16)],
        compiler_params=_params(2),
        name="norm_in_proj",
    )(x, g, w)


def _mixer_kernel(xr_ref, gr_ref, v_ref, cw_ref, cb_ref, wg_ref, ba_ref, bx_ref, lam_ref,
                  beta_ref, wp_ref, bp_ref, ps_ref, o_ref,
                  xbuf, vbuf, a_s, b_s, hst, y_s, *, tile, n_slabs, pool_group):
    t = pl.program_id(1)
    d_rnn = n_slabs * LANES
    pitch = n_slabs + 1
    halo_x = SUBLANES
    halo_v = 2 * SUBLANES

    @pl.when(t == 0)
    def _():
        xbuf[0:halo_x, :] = jnp.zeros((halo_x, d_rnn), F32)
        vbuf[0:halo_v, :] = jnp.zeros((halo_v, vbuf.shape[1]), F32)
        hst[...] = jnp.zeros(hst.shape, F32)

    xbuf[halo_x:halo_x + tile, :] = xr_ref[...]
    vbuf[halo_v:halo_v + tile, :] = v_ref[...]

    for c in range(n_slabs):
        sl = slice(c * LANES, (c + 1) * LANES)
        xc = cb_ref[:, sl]
        for k in range(CONV_WIDTH):
            off = halo_x - (CONV_WIDTH - 1) + k
            xc = xc + cw_ref[k:k + 1, sl] * xbuf[off:off + tile, sl]
        pre = jnp.dot(xc.astype(BF16), wg_ref[c], preferred_element_type=F32)
        r = jax.nn.sigmoid(pre[:, :LANES] + ba_ref[:, sl])
        i = jax.nn.sigmoid(pre[:, LANES:] + bx_ref[:, sl])
        neg_lam = -lam_ref[:, sl]
        softplus = jnp.maximum(neg_lam, 0.0) + jnp.log1p(jnp.exp(-jnp.abs(neg_lam)))
        log_a = -LRU_C * r * softplus
        a = jnp.exp(log_a)
        th = jnp.tanh(log_a)
        mult = jnp.sqrt(-2.0 * th / (1.0 - th))
        a_s[pl.ds(c, tile, stride=pitch), :] = a
        b_s[pl.ds(c, tile, stride=pitch), :] = mult * i * xc

    groups = n_slabs // SUBLANES

    def step(tt, hs):
        base = tt * pitch
        out = []
        for q in range(groups):
            rows = pl.ds(base + q * SUBLANES, SUBLANES)
            h = a_s[rows, :] * hs[q] + b_s[rows, :]
            b_s[rows, :] = h
            out.append(h)
        return tuple(out)

    h0 = tuple(hst[q * SUBLANES:(q + 1) * SUBLANES, :] for q in range(groups))
    hN = lax.fori_loop(0, tile, step, h0, unroll=8)
    for q in range(groups):
        hst[q * SUBLANES:(q + 1) * SUBLANES, :] = hN[q]

    for c in range(n_slabs):
        y_s[:, c * LANES:(c + 1) * LANES] = b_s[pl.ds(c, tile, stride=pitch), :]
    y = y_s[...] * jax.nn.gelu(gr_ref[...], approximate=True)
    o_ref[:, 0:d_rnn] = _rms_scale(y, beta_ref[...]).astype(BF16)

    pos = t * tile + lax.broadcasted_iota(jnp.int32, (tile, 1), 0) + 1
    for g, w in enumerate(POOL_WINDOWS):
        gl = slice(g * pool_group, (g + 1) * pool_group)
        cur = vbuf[halo_v:halo_v + tile, gl]
        win = cur
        for j in range(1, w):
            win = win + vbuf[halo_v - j:halo_v - j + tile, gl]
        cnt = jnp.minimum(pos, w).astype(F32)
        z = win / cnt - cur
        y_s[:, gl] = (jnp.dot(z.astype(BF16), wp_ref[g], preferred_element_type=F32)
                      + bp_ref[:, gl])
    o_ref[:, d_rnn:] = _rms_scale(y_s[...], ps_ref[...]).astype(BF16)

    xbuf[0:halo_x, :] = xbuf[tile:tile + halo_x, :]
    vbuf[0:halo_v, :] = vbuf[tile:tile + halo_v, :]


def _mixer(proj, cw, cb, wg, ba, bx, lam, beta, wp, bp, ps, *, tile, d_rnn, d_pool):
    b, s, _ = proj.shape
    assert d_rnn == d_pool
    n_slabs = d_rnn // LANES
    pitch = n_slabs + 1
    pool_group = d_pool // len(POOL_WINDOWS)
    vec = lambda width: pl.BlockSpec((1, width), lambda i, j: (0, 0))
    kern = functools.partial(_mixer_kernel, tile=tile, n_slabs=n_slabs, pool_group=pool_group)
    return pl.pallas_call(
        kern,
        grid=(b, s // tile),
        in_specs=[
            pl.BlockSpec((None, tile, d_rnn), lambda i, j: (i, j, 0)),
            pl.BlockSpec((None, tile, d_rnn), lambda i, j: (i, j, 1)),
            pl.BlockSpec((None, tile, d_pool), lambda i, j: (i, j, 2)),
            pl.BlockSpec((CONV_WIDTH, d_rnn), lambda i, j: (0, 0)),
            vec(d_rnn),
            pl.BlockSpec(wg.shape, lambda i, j: (0, 0, 0)),
            vec(d_rnn), vec(d_rnn), vec(d_rnn), vec(d_rnn),
            pl.BlockSpec(wp.shape, lambda i, j: (0, 0, 0)),
            vec(d_pool), vec(d_pool),
        ],
        out_specs=pl.BlockSpec((None, tile, d_rnn + d_pool), lambda i, j: (i, j, 0)),
        out_shape=jax.ShapeDtypeStruct((b, s, d_rnn + d_pool), BF16),
        scratch_shapes=[
            pltpu.VMEM((tile + SUBLANES, d_rnn), F32),
            pltpu.VMEM((tile + 2 * SUBLANES, d_pool), F32),
            pltpu.VMEM((tile * pitch, LANES), F32),
            pltpu.VMEM((tile * pitch, LANES), F32),
            pltpu.VMEM((n_slabs, LANES), F32),
            pltpu.VMEM((tile, d_rnn), F32),
        ],
        compiler_params=_params(2),
        name="mixer",
    )(proj, proj, proj, cw, cb, wg, ba, bx, lam, beta, wp, bp, ps)


def _proj_res_kernel(a_ref, w_ref, r_ref, o_ref):
    o_ref[...] = r_ref[...] + jnp.dot(a_ref[...], w_ref[...], preferred_element_type=F32)


def _proj_res(a, w, res, *, bm, bn):
    m, k = a.shape
    n = w.shape[1]
    return pl.pallas_call(
        _proj_res_kernel,
        grid=(m // bm, n // bn),
        in_specs=[
            pl.BlockSpec((bm, k), lambda i, j: (i, 0)),
            pl.BlockSpec((k, bn), lambda i, j: (0, j)),
            pl.BlockSpec((bm, bn), lambda i, j: (i, j)),
        ],
        out_specs=pl.BlockSpec((bm, bn), lambda i, j: (i, j)),
        out_shape=jax.ShapeDtypeStruct((m, n), F32),
        compiler_params=_params(2),
        name="out_proj",
    )(a, w, res)


def _mlp_kernel(h_ref, g_ref, wu_ref, wd_ref, o_ref, u_ref):
    @pl.when(pl.program_id(1) == 0)
    def _():
        h = h_ref[...]
        u_ref[...] = _rms_scale(h, g_ref[...]).astype(BF16)
        o_ref[...] = h

    a = jnp.dot(u_ref[...], wu_ref[...], preferred_element_type=F32)
    a = jnp.square(jnp.maximum(a, 0.0))
    o_ref[...] += jnp.dot(a.astype(BF16), wd_ref[...], preferred_element_type=F32)


def _mlp(h, g, wu, wd, *, bm, bf):
    m, d = h.shape
    f = wu.shape[1]
    return pl.pallas_call(
        _mlp_kernel,
        grid=(m // bm, f // bf),
        in_specs=[
            pl.BlockSpec((bm, d), lambda i, j: (i, 0), pipeline_mode=pl.Buffered(1)),
            pl.BlockSpec((1, d), lambda i, j: (0, 0)),
            pl.BlockSpec((d, bf), lambda i, j: (0, j)),
            pl.BlockSpec((bf, d), lambda i, j: (j, 0)),
        ],
        out_specs=pl.BlockSpec((bm, d), lambda i, j: (i, 0)),
        out_shape=jax.ShapeDtypeStruct((m, d), F32),
        scratch_shapes=[pltpu.VMEM((bm, d), BF16)],
        compiler_params=_params(2),
        name="mlp",
    )(h, g, wu, wd)


def _ple_kernel(h_ref, ht_ref, g_ref, wg_ref, p_ref, wp_ref, o_ref, u_ref):
    @pl.when(pl.program_id(1) == 0)
    def _():
        u_ref[...] = _rms_scale(h_ref[...], g_ref[...]).astype(BF16)

    gate = jax.nn.sigmoid(jnp.dot(u_ref[...], wg_ref[...], preferred_element_type=F32))
    emb = jnp.dot(p_ref[...].astype(BF16), wp_ref[...], preferred_element_type=F32)
    o_ref[...] = ht_ref[...] + gate * emb


def _ple(h, g, wg, p, wp, *, bm, bn):
    m, d = h.shape
    n = wg.shape[1]
    e = p.shape[1]
    return pl.pallas_call(
        _ple_kernel,
        grid=(m // bm, n // bn),
        in_specs=[
            pl.BlockSpec((bm, d), lambda i, j: (i, 0)),
            pl.BlockSpec((bm, bn), lambda i, j: (i, j)),
            pl.BlockSpec((1, d), lambda i, j: (0, 0)),
            pl.BlockSpec((d, bn), lambda i, j: (0, j)),
            pl.BlockSpec((bm, e), lambda i, j: (i, 0)),
            pl.BlockSpec((e, bn), lambda i, j: (0, j)),
        ],
        out_specs=pl.BlockSpec((bm, bn), lambda i, j: (i, j)),
        out_shape=jax.ShapeDtypeStruct((m, n), F32),
        scratch_shapes=[pltpu.VMEM((bm, d), BF16)],
        compiler_params=_params(2),
        name="ple",
    )(h, h, g, wg, p, wp)


def _final_norm_kernel(h_ref, g_ref, o_ref):
    o_ref[...] = _rms_scale(h_ref[...], g_ref[...])


def _final_norm(h, g, *, bm):
    m, d = h.shape
    return pl.pallas_call(
        _final_norm_kernel,
        grid=(m // bm,),
        in_specs=[pl.BlockSpec((bm, d), lambda i: (i, 0)), pl.BlockSpec((1, d), lambda i: (0, 0))],
        out_specs=pl.BlockSpec((bm, d), lambda i: (i, 0)),
        out_shape=jax.ShapeDtypeStruct((m, d), F32),
        compiler_params=_params(1),
        name="final_norm",
    )(h, g)


def kernel(x, p, norm_mix_g, w_in, conv_w, conv_b, w_rg_a, b_rg_a, w_rg_x, b_rg_x, lru_lambda,
           beta_rnn, w_pool, b_pool, pool_scale, w_out, norm_mlp_g, w_up, w_down, norm_ple_g,
           w_ple_gate, w_ple_proj, final_norm_g):
    bsz, seq, d_model = x.shape
    depth = w_in.shape[0]
    d_rnn = conv_w.shape[-1]
    d_pool = b_pool.shape[-1]
    m = bsz * seq
    row = lambda v: v.reshape(1, -1)

    h = x.reshape(m, d_model)
    for l in range(depth):
        proj = _norm_proj(h, row(norm_mix_g[l]), w_in[l].astype(BF16), bm=512, bn=1024)
        w_gates = jnp.concatenate([w_rg_a[l], w_rg_x[l]], axis=-1).astype(BF16)
        mix = _mixer(
            proj.reshape(bsz, seq, -1), conv_w[l], row(conv_b[l]), w_gates, row(b_rg_a[l]),
            row(b_rg_x[l]), row(lru_lambda[l]), row(beta_rnn[l]), w_pool[l].astype(BF16),
            row(b_pool[l]), row(pool_scale[l]), tile=256, d_rnn=d_rnn, d_pool=d_pool)
        h = _proj_res(mix.reshape(m, -1), w_out[l].astype(BF16), h, bm=512, bn=1024)
        h = _mlp(h, row(norm_mlp_g[l]), w_up[l].astype(BF16), w_down[l].astype(BF16),
                 bm=512, bf=512)
        h = _ple(h, row(norm_ple_g[l]), w_ple_gate[l].astype(BF16), p[l].reshape(m, -1),
                 w_ple_proj[l].astype(BF16), bm=512, bn=1024)
    out = _final_norm(h, row(final_norm_g), bm=256)
    return out.reshape(bsz, seq, d_model)
```

```python
import functools

import jax
import jax.numpy as jnp
from jax import lax
from jax.experimental import pallas as pl
from jax.experimental.pallas import tpu as pltpu

F32 = jnp.float32
BF16 = jnp.bfloat16

EPS = 1e-6
LRU_C = 8.0
CONV_WIDTH = 4
POOL_WINDOWS = (2, 4, 8, 16)
LANES = 128
SUBLANES = 8
BF16_ROWS = 16
VMEM_LIMIT = 60 * 1024 * 1024


def _rms_scale(x, g):
    ms = jnp.mean(x * x, axis=-1, keepdims=True)
    return x * lax.rsqrt(ms + EPS) * g


def _norm_rows_bf16(dst_ref, src_ref, g_ref, rs_ref):
    rows, d = src_ref.shape
    lane_tiles = [slice(c * LANES, (c + 1) * LANES) for c in range(d // LANES)]

    def sumsq(i, carry):
        r = pl.ds(pl.multiple_of(i * SUBLANES, SUBLANES), SUBLANES)
        sq = [jnp.square(src_ref[r, cs]) for cs in lane_tiles]
        while len(sq) > 1:
            sq = [a + b for a, b in zip(sq[0::2], sq[1::2])]
        rs_ref[r, :] = sq[0]
        return carry

    lax.fori_loop(0, rows // SUBLANES, sumsq, 0, unroll=2)
    ms = jnp.sum(rs_ref[...], axis=-1, keepdims=True) * (1.0 / d)
    rs_ref[...] = jnp.broadcast_to(lax.rsqrt(ms + EPS), rs_ref.shape)

    def scale(i, carry):
        r = pl.ds(pl.multiple_of(i * BF16_ROWS, BF16_ROWS), BF16_ROWS)
        rs = rs_ref[r, :]
        for cs in lane_tiles:
            dst_ref[r, cs] = (src_ref[r, cs] * rs * g_ref[:, cs]).astype(BF16)
        return carry

    lax.fori_loop(0, rows // BF16_ROWS, scale, 0, unroll=2)


def _params(n_axes):
    return pltpu.CompilerParams(
        dimension_semantics=("arbitrary",) * n_axes, vmem_limit_bytes=VMEM_LIMIT)


def _norm_proj_kernel(x_ref, g_ref, w_ref, o_ref, u_ref, rs_ref):
    @pl.when(pl.program_id(1) == 0)
    def _():
        _norm_rows_bf16(u_ref, x_ref, g_ref, rs_ref)

    o_ref[...] = jnp.dot(u_ref[...], w_ref[...], preferred_element_type=F32)


def _norm_proj(x, g, w, *, bm, bn):
    m, d = x.shape
    n = w.shape[1]
    return pl.pallas_call(
        _norm_proj_kernel,
        grid=(m // bm, n // bn),
        in_specs=[
            pl.BlockSpec((bm, d), lambda i, j: (i, 0)),
            pl.BlockSpec((1, d), lambda i, j: (0, 0)),
            pl.BlockSpec((d, bn), lambda i, j: (0, j)),
        ],
        out_specs=pl.BlockSpec((bm, bn), lambda i, j: (i, j)),
        out_shape=jax.ShapeDtypeStruct((m, n), F32),
        scratch_shapes=[pltpu.VMEM((bm, d), BF16), pltpu.VMEM((bm, LANES), F32)],
        compiler_params=_params(2),
        name="norm_in_proj",
    )(x, g, w)


def _mixer_kernel(xr_ref, gr_ref, v_ref, cw_ref, cb_ref, wg_ref, ba_ref, bx_ref, lam_ref,
                  beta_ref, wp_ref, bp_ref, ps_ref, o_ref,
                  xt, ct, a_t, b_t, hst, vt, cst, zt, y_s, *, tile, n_slabs, halo_x, halo_v):
    t_idx = pl.program_id(1)
    d_rnn = n_slabs * LANES
    pitch = n_slabs + 1
    groups = n_slabs // SUBLANES

    def slab_rows(c, t0=0):
        return pl.ds(t0 * pitch + c, tile, stride=pitch)

    def step_rows(step, q):
        return pl.ds(step * pitch + q * SUBLANES, SUBLANES)

    @pl.when(t_idx == 0)
    def _():
        xt[0:halo_x * pitch, :] = jnp.zeros((halo_x * pitch, LANES), F32)
        vt[0:halo_v * pitch, :] = jnp.zeros((halo_v * pitch, LANES), F32)
        hst[...] = jnp.zeros(hst.shape, F32)

    for c in range(n_slabs):
        sl = slice(c * LANES, (c + 1) * LANES)
        xt[slab_rows(c, halo_x), :] = xr_ref[:, sl]
        vt[slab_rows(c, halo_v), :] = v_ref[:, sl]

    cw = [[cw_ref[k, q * SUBLANES:(q + 1) * SUBLANES, :] for q in range(groups)]
          for k in range(CONV_WIDTH)]
    cb = [cb_ref[q * SUBLANES:(q + 1) * SUBLANES, :] for q in range(groups)]

    def conv_step(t, carry):
        for q in range(groups):
            acc = cb[q]
            for k in range(CONV_WIDTH):
                acc = acc + cw[k][q] * xt[step_rows(t + halo_x - (CONV_WIDTH - 1) + k, q), :]
            ct[step_rows(t, q), :] = acc
        return carry

    lax.fori_loop(0, tile, conv_step, 0, unroll=8)

    for c in range(n_slabs):
        sl = slice(c * LANES, (c + 1) * LANES)
        xc = ct[slab_rows(c), :]
        pre = jnp.dot(xc.astype(BF16), wg_ref[c], preferred_element_type=F32)
        th_r = jnp.tanh(0.5 * (pre[:, :LANES] + ba_ref[:, sl]))
        th_i = jnp.tanh(0.5 * (pre[:, LANES:] + bx_ref[:, sl]))
        neg_lam = -lam_ref[:, sl]
        softplus = jnp.maximum(neg_lam, 0.0) + jnp.log1p(jnp.exp(-jnp.abs(neg_lam)))
        log_a = (-0.5 * LRU_C * softplus) * (th_r + 1.0)
        th = jnp.tanh(log_a)
        m2 = -2.0 * th / (1.0 - th)
        mult = jnp.where(m2 > 0.0, m2 * lax.rsqrt(m2), 0.0)
        a_t[slab_rows(c), :] = jnp.exp(log_a)
        b_t[slab_rows(c), :] = mult * (0.5 * (th_i + 1.0)) * xc

    def scan_step(t, hs):
        out = []
        for q in range(groups):
            rows = step_rows(t, q)
            h = a_t[rows, :] * hs[q] + b_t[rows, :]
            b_t[rows, :] = h
            out.append(h)
        return tuple(out)

    h0 = tuple(hst[q * SUBLANES:(q + 1) * SUBLANES, :] for q in range(groups))
    h_last = lax.fori_loop(0, tile, scan_step, h0, unroll=8)
    for q in range(groups):
        hst[q * SUBLANES:(q + 1) * SUBLANES, :] = h_last[q]

    for c in range(n_slabs):
        y_s[:, c * LANES:(c + 1) * LANES] = b_t[slab_rows(c), :]
    y = y_s[...] * jax.nn.gelu(gr_ref[...], approximate=True)
    o_ref[:, 0:d_rnn] = _rms_scale(y, beta_ref[...]).astype(BF16)

    zeros = jnp.zeros((SUBLANES, LANES), F32)
    for q in range(groups):
        cst[step_rows(0, q), :] = zeros

    def cumsum_step(u, cs):
        out = []
        for q in range(groups):
            c = cs[q] + vt[step_rows(u - 1, q), :]
            cst[step_rows(u, q), :] = c
            out.append(c)
        return tuple(out)

    lax.fori_loop(1, tile + halo_v + 1, cumsum_step, (zeros,) * groups, unroll=8)

    low = lax.broadcasted_iota(jnp.int32, (SUBLANES, LANES), 0) < SUBLANES // 2
    w_lo = [POOL_WINDOWS[2 * q] for q in range(groups)]
    w_hi = [POOL_WINDOWS[2 * q + 1] for q in range(groups)]
    w_vec = [jnp.where(low, float(w_lo[q]), float(w_hi[q])) for q in range(groups)]

    def window_step(t, scale):
        u = t + halo_v + 1
        for q in range(groups):
            prev = jnp.where(low, cst[step_rows(u - w_lo[q], q), :],
                             cst[step_rows(u - w_hi[q], q), :])
            win = cst[step_rows(u, q), :] - prev
            zt[step_rows(t, q), :] = scale(win, t, q) - vt[step_rows(t + halo_v, q), :]

    def head_scale(win, t, q):
        pos = (t_idx * tile + t + 1).astype(F32)
        return win / jnp.minimum(pos, w_vec[q])

    def body_scale(win, t, q):
        return win * (1.0 / w_vec[q])

    def head_step(t, carry):
        window_step(t, head_scale)
        return carry

    def body_step(t, carry):
        window_step(t, body_scale)
        return carry

    ramp = max(POOL_WINDOWS)
    lax.fori_loop(0, ramp, head_step, 0, unroll=4)
    lax.fori_loop(ramp, tile, body_step, 0, unroll=8)

    slabs_per_group = n_slabs // len(POOL_WINDOWS)
    for g in range(len(POOL_WINDOWS)):
        z = jnp.concatenate(
            [zt[slab_rows(g * slabs_per_group + s), :] for s in range(slabs_per_group)], axis=1)
        gl = slice(g * slabs_per_group * LANES, (g + 1) * slabs_per_group * LANES)
        y_s[:, gl] = (jnp.dot(z.astype(BF16), wp_ref[g], preferred_element_type=F32)
                      + bp_ref[:, gl])
    o_ref[:, d_rnn:] = _rms_scale(y_s[...], ps_ref[...]).astype(BF16)

    xt[0:halo_x * pitch, :] = xt[tile * pitch:(tile + halo_x) * pitch, :]
    vt[0:halo_v * pitch, :] = vt[tile * pitch:(tile + halo_v) * pitch, :]


def _mixer(proj, cw, cb, wg, ba, bx, lam, beta, wp, bp, ps, *, tile, d_rnn, d_pool):
    b, s, _ = proj.shape
    n_slabs = d_rnn // LANES
    assert d_rnn == d_pool and n_slabs % SUBLANES == 0
    assert n_slabs // len(POOL_WINDOWS) == SUBLANES // 2
    pitch = n_slabs + 1
    halo_x = SUBLANES
    halo_v = max(POOL_WINDOWS)
    assert halo_x >= CONV_WIDTH - 1 and (tile * pitch) % SUBLANES == 0
    vec = lambda width: pl.BlockSpec((1, width), lambda i, j: (0, 0))
    time_major = lambda steps: pltpu.VMEM((steps * pitch, LANES), F32)
    kern = functools.partial(_mixer_kernel, tile=tile, n_slabs=n_slabs, halo_x=halo_x,
                             halo_v=halo_v)
    return pl.pallas_call(
        kern,
        grid=(b, s // tile),
        in_specs=[
            pl.BlockSpec((None, tile, d_rnn), lambda i, j: (i, j, 0)),
            pl.BlockSpec((None, tile, d_rnn), lambda i, j: (i, j, 1)),
            pl.BlockSpec((None, tile, d_pool), lambda i, j: (i, j, 2)),
            pl.BlockSpec((CONV_WIDTH, n_slabs, LANES), lambda i, j: (0, 0, 0)),
            pl.BlockSpec((n_slabs, LANES), lambda i, j: (0, 0)),
            pl.BlockSpec(wg.shape, lambda i, j: (0, 0, 0)),
            vec(d_rnn), vec(d_rnn), vec(d_rnn), vec(d_rnn),
            pl.BlockSpec(wp.shape, lambda i, j: (0, 0, 0)),
            vec(d_pool), vec(d_pool),
        ],
        out_specs=pl.BlockSpec((None, tile, d_rnn + d_pool), lambda i, j: (i, j, 0)),
        out_shape=jax.ShapeDtypeStruct((b, s, d_rnn + d_pool), BF16),
        scratch_shapes=[
            time_major(tile + halo_x),
            time_major(tile),
            time_major(tile),
            time_major(tile),
            pltpu.VMEM((n_slabs, LANES), F32),
            time_major(tile + halo_v),
            time_major(tile + halo_v + 1),
            time_major(tile),
            pltpu.VMEM((tile, d_rnn), F32),
        ],
        compiler_params=_params(2),
        name="mixer",
    )(proj, proj, proj, cw.reshape(CONV_WIDTH, n_slabs, LANES), cb.reshape(n_slabs, LANES),
      wg, ba, bx, lam, beta, wp, bp, ps)


def _proj_res_kernel(a_ref, w_ref, r_ref, o_ref):
    o_ref[...] = r_ref[...] + jnp.dot(a_ref[...], w_ref[...], preferred_element_type=F32)


def _proj_res(a, w, res, *, bm, bn):
    m, k = a.shape
    n = w.shape[1]
    return pl.pallas_call(
        _proj_res_kernel,
        grid=(m // bm, n // bn),
        in_specs=[
            pl.BlockSpec((bm, k), lambda i, j: (i, 0)),
            pl.BlockSpec((k, bn), lambda i, j: (0, j)),
            pl.BlockSpec((bm, bn), lambda i, j: (i, j)),
        ],
        out_specs=pl.BlockSpec((bm, bn), lambda i, j: (i, j)),
        out_shape=jax.ShapeDtypeStruct((m, n), F32),
        compiler_params=_params(2),
        name="out_proj",
    )(a, w, res)


def _mlp_kernel(h_ref, g_ref, g2_ref, wu_ref, wd_ref, o_ref, u2_ref, u_ref, rs_ref):
    j = pl.program_id(1)

    @pl.when(j == 0)
    def _():
        _norm_rows_bf16(u_ref, h_ref, g_ref, rs_ref)
        o_ref[...] = h_ref[...]

    a = jnp.dot(u_ref[...], wu_ref[...], preferred_element_type=F32)
    a = jnp.square(jnp.maximum(a, 0.0))
    o_ref[...] += jnp.dot(a.astype(BF16), wd_ref[...], preferred_element_type=F32)

    @pl.when(j == pl.num_programs(1) - 1)
    def _():
        _norm_rows_bf16(u2_ref, o_ref, g2_ref, rs_ref)


def _mlp(h, g, g2, wu, wd, *, bm, bf):
    m, d = h.shape
    f = wu.shape[1]
    row_block = lambda i, j: (i, 0)
    return pl.pallas_call(
        _mlp_kernel,
        grid=(m // bm, f // bf),
        in_specs=[
            pl.BlockSpec((bm, d), row_block, pipeline_mode=pl.Buffered(1)),
            pl.BlockSpec((1, d), lambda i, j: (0, 0)),
            pl.BlockSpec((1, d), lambda i, j: (0, 0)),
            pl.BlockSpec((d, bf), lambda i, j: (0, j)),
            pl.BlockSpec((bf, d), lambda i, j: (j, 0)),
        ],
        out_specs=[pl.BlockSpec((bm, d), row_block), pl.BlockSpec((bm, d), row_block)],
        out_shape=[jax.ShapeDtypeStruct((m, d), F32), jax.ShapeDtypeStruct((m, d), BF16)],
        scratch_shapes=[pltpu.VMEM((bm, d), BF16), pltpu.VMEM((bm, LANES), F32)],
        compiler_params=_params(2),
        name="mlp",
    )(h, g, g2, wu, wd)


def _ple_kernel(u_ref, h_ref, wg_ref, p_ref, wp_ref, o_ref):
    pb = p_ref[...].astype(BF16)
    bn = o_ref.shape[1]
    cols = min(bn, 512)
    for c0 in range(0, bn, cols):
        cs = slice(c0, c0 + cols)
        z = jnp.dot(u_ref[...], wg_ref[:, cs], preferred_element_type=F32)
        gate = 0.5 * jnp.tanh(0.5 * z) + 0.5
        emb = jnp.dot(pb, wp_ref[:, cs], preferred_element_type=F32)
        o_ref[:, cs] = h_ref[:, cs] + gate * emb


def _ple(u, h, wg, p, wp, *, bm, bn):
    m, d = u.shape
    n = wg.shape[1]
    e = p.shape[1]
    return pl.pallas_call(
        _ple_kernel,
        grid=(m // bm, n // bn),
        in_specs=[
            pl.BlockSpec((bm, d), lambda i, j: (i, 0)),
            pl.BlockSpec((bm, bn), lambda i, j: (i, j)),
            pl.BlockSpec((d, bn), lambda i, j: (0, j)),
            pl.BlockSpec((bm, e), lambda i, j: (i, 0)),
            pl.BlockSpec((e, bn), lambda i, j: (0, j)),
        ],
        out_specs=pl.BlockSpec((bm, bn), lambda i, j: (i, j)),
        out_shape=jax.ShapeDtypeStruct((m, n), F32),
        compiler_params=_params(2),
        name="ple",
    )(u, h, wg, p, wp)


def _final_norm_kernel(h_ref, g_ref, o_ref):
    o_ref[...] = _rms_scale(h_ref[...], g_ref[...])


def _final_norm(h, g, *, bm):
    m, d = h.shape
    return pl.pallas_call(
        _final_norm_kernel,
        grid=(m // bm,),
        in_specs=[pl.BlockSpec((bm, d), lambda i: (i, 0)), pl.BlockSpec((1, d), lambda i: (0, 0))],
        out_specs=pl.BlockSpec((bm, d), lambda i: (i, 0)),
        out_shape=jax.ShapeDtypeStruct((m, d), F32),
        compiler_params=_params(1),
        name="final_norm",
    )(h, g)


def kernel(x, p, norm_mix_g, w_in, conv_w, conv_b, w_rg_a, b_rg_a, w_rg_x, b_rg_x, lru_lambda,
           beta_rnn, w_pool, b_pool, pool_scale, w_out, norm_mlp_g, w_up, w_down, norm_ple_g,
           w_ple_gate, w_ple_proj, final_norm_g):
    bsz, seq, d_model = x.shape
    depth = w_in.shape[0]
    d_rnn = conv_w.shape[-1]
    d_pool = b_pool.shape[-1]
    m = bsz * seq
    row = lambda v: v.reshape(1, -1)

    h = x.reshape(m, d_model)
    for l in range(depth):
        proj = _norm_proj(h, row(norm_mix_g[l]), w_in[l].astype(BF16), bm=1024, bn=512)
        w_gates = jnp.concatenate([w_rg_a[l], w_rg_x[l]], axis=-1).astype(BF16)
        mix = _mixer(
            proj.reshape(bsz, seq, -1), conv_w[l], conv_b[l], w_gates, row(b_rg_a[l]),
            row(b_rg_x[l]), row(lru_lambda[l]), row(beta_rnn[l]), w_pool[l].astype(BF16),
            row(b_pool[l]), row(pool_scale[l]), tile=256, d_rnn=d_rnn, d_pool=d_pool)
        h = _proj_res(mix.reshape(m, -1), w_out[l].astype(BF16), h, bm=1024, bn=1024)
        h, u = _mlp(h, row(norm_mlp_g[l]), row(norm_ple_g[l]), w_up[l].astype(BF16),
                    w_down[l].astype(BF16), bm=512, bf=512)
        h = _ple(u, h, w_ple_gate[l].astype(BF16), p[l].reshape(m, -1),
                 w_ple_proj[l].astype(BF16), bm=1024, bn=1024)
    out = _final_norm(h, row(final_norm_g), bm=256)
    return out.reshape(bsz, seq, d_model)
```

```python
import functools

import jax
import jax.numpy as jnp
from jax import lax
from jax.experimental import pallas as pl
from jax.experimental.pallas import tpu as pltpu

F32 = jnp.float32
BF16 = jnp.bfloat16

EPS = 1e-6
LRU_C = 8.0
CONV_WIDTH = 4
POOL_WINDOWS = (2, 4, 8, 16)
LANES = 128
SUBLANES = 8
BF16_ROWS = 16
VMEM_LIMIT = 60 * 1024 * 1024
COL_CHUNK = 512


def _rms_scale(x, g):
    ms = jnp.mean(x * x, axis=-1, keepdims=True)
    return x * lax.rsqrt(ms + EPS) * g


def _params(n_axes):
    return pltpu.CompilerParams(
        dimension_semantics=("arbitrary",) * n_axes, vmem_limit_bytes=VMEM_LIMIT)


def _lane_tiles(width):
    return [slice(c * LANES, (c + 1) * LANES) for c in range(width // LANES)]


def _col_chunks(width):
    step = min(width, COL_CHUNK)
    return [slice(c, c + step) for c in range(0, width, step)]


def _tree_sum(parts):
    while len(parts) > 1:
        parts = [a + b for a, b in zip(parts[0::2], parts[1::2])] + parts[len(parts) & ~1:]
    return parts[0]


def _lane_partial_sumsq(h):
    return _tree_sum([jnp.square(h[:, cs]) for cs in _lane_tiles(h.shape[1])])


def _row_scale(ss_ref, rs_ref, width):
    ms = jnp.sum(ss_ref[...], axis=-1, keepdims=True) * (1.0 / width)
    rs_ref[...] = jnp.broadcast_to(lax.rsqrt(ms + EPS), rs_ref.shape)


def _scale_rows(z, rs):
    return jnp.concatenate([z[:, cs] * rs for cs in _lane_tiles(z.shape[1])], axis=1)


def _norm_proj_kernel(x_ref, g_ref, w_ref, o_ref, xb_ref, rs_ref):
    rows, d = x_ref.shape

    @pl.when(pl.program_id(1) == 0)
    def _():
        def chunk(i, carry):
            r = pl.ds(pl.multiple_of(i * BF16_ROWS, BF16_ROWS), BF16_ROWS)
            sq = []
            for cs in _lane_tiles(d):
                x = x_ref[r, cs]
                xb_ref[r, cs] = (x * g_ref[:, cs]).astype(BF16)
                sq.append(jnp.square(x))
            rs_ref[r, :] = _tree_sum(sq)
            return carry

        lax.fori_loop(0, rows // BF16_ROWS, chunk, 0, unroll=2)
        _row_scale(rs_ref, rs_ref, d)

    z = jnp.dot(xb_ref[...], w_ref[...], preferred_element_type=F32)
    o_ref[...] = _scale_rows(z, rs_ref[...])


def _norm_proj(x, g, w, *, bm, bn):
    m, d = x.shape
    n = w.shape[1]
    return pl.pallas_call(
        _norm_proj_kernel,
        grid=(m // bm, n // bn),
        in_specs=[
            pl.BlockSpec((bm, d), lambda i, j: (i, 0)),
            pl.BlockSpec((1, d), lambda i, j: (0, 0)),
            pl.BlockSpec((d, bn), lambda i, j: (0, j)),
        ],
        out_specs=pl.BlockSpec((bm, bn), lambda i, j: (i, j)),
        out_shape=jax.ShapeDtypeStruct((m, n), F32),
        scratch_shapes=[pltpu.VMEM((bm, d), BF16), pltpu.VMEM((bm, LANES), F32)],
        compiler_params=_params(2),
        name="norm_in_proj",
    )(x, g, w)


def _mixer_kernel(xr_ref, gr_ref, v_ref, cw_ref, cb_ref, wg_ref, ba_ref, bx_ref, lam_ref,
                  beta_ref, wp_ref, bp_ref, ps_ref, o_ref,
                  xt, ct, a_t, b_t, hst, vt, cst, zt, y_s, *, tile, n_slabs, halo_x, halo_v):
    t_idx = pl.program_id(1)
    d_rnn = n_slabs * LANES
    pitch = n_slabs + 1
    groups = n_slabs // SUBLANES

    def slab_rows(c, t0=0):
        return pl.ds(t0 * pitch + c, tile, stride=pitch)

    def step_rows(step, q):
        return pl.ds(step * pitch + q * SUBLANES, SUBLANES)

    @pl.when(t_idx == 0)
    def _():
        xt[0:halo_x * pitch, :] = jnp.zeros((halo_x * pitch, LANES), F32)
        vt[0:halo_v * pitch, :] = jnp.zeros((halo_v * pitch, LANES), F32)
        hst[...] = jnp.zeros(hst.shape, F32)

    for c in range(n_slabs):
        sl = slice(c * LANES, (c + 1) * LANES)
        xt[slab_rows(c, halo_x), :] = xr_ref[:, sl]
        vt[slab_rows(c, halo_v), :] = v_ref[:, sl]

    cw = [[cw_ref[k, q * SUBLANES:(q + 1) * SUBLANES, :] for q in range(groups)]
          for k in range(CONV_WIDTH)]
    cb = [cb_ref[q * SUBLANES:(q + 1) * SUBLANES, :] for q in range(groups)]

    def conv_step(t, carry):
        for q in range(groups):
            acc = cb[q]
            for k in range(CONV_WIDTH):
                acc = acc + cw[k][q] * xt[step_rows(t + halo_x - (CONV_WIDTH - 1) + k, q), :]
            ct[step_rows(t, q), :] = acc
        return carry

    lax.fori_loop(0, tile, conv_step, 0, unroll=8)

    for c in range(n_slabs):
        sl = slice(c * LANES, (c + 1) * LANES)
        xc = ct[slab_rows(c), :]
        pre = jnp.dot(xc.astype(BF16), wg_ref[c], preferred_element_type=F32)
        th_r = jnp.tanh(0.5 * (pre[:, :LANES] + ba_ref[:, sl]))
        th_i = jnp.tanh(0.5 * (pre[:, LANES:] + bx_ref[:, sl]))
        neg_lam = -lam_ref[:, sl]
        softplus = jnp.maximum(neg_lam, 0.0) + jnp.log1p(jnp.exp(-jnp.abs(neg_lam)))
        log_a = (-0.5 * LRU_C * softplus) * (th_r + 1.0)
        th = jnp.tanh(log_a)
        m2 = -2.0 * th / (1.0 - th)
        mult = jnp.where(m2 > 0.0, m2 * lax.rsqrt(m2), 0.0)
        a_t[slab_rows(c), :] = jnp.exp(log_a)
        b_t[slab_rows(c), :] = mult * (0.5 * (th_i + 1.0)) * xc

    def scan_step(t, hs):
        out = []
        for q in range(groups):
            rows = step_rows(t, q)
            h = a_t[rows, :] * hs[q] + b_t[rows, :]
            b_t[rows, :] = h
            out.append(h)
        return tuple(out)

    h0 = tuple(hst[q * SUBLANES:(q + 1) * SUBLANES, :] for q in range(groups))
    h_last = lax.fori_loop(0, tile, scan_step, h0, unroll=8)
    for q in range(groups):
        hst[q * SUBLANES:(q + 1) * SUBLANES, :] = h_last[q]

    for c in range(n_slabs):
        y_s[:, c * LANES:(c + 1) * LANES] = b_t[slab_rows(c), :]
    y = y_s[...] * jax.nn.gelu(gr_ref[...], approximate=True)
    o_ref[:, 0:d_rnn] = _rms_scale(y, beta_ref[...]).astype(BF16)

    zeros = jnp.zeros((SUBLANES, LANES), F32)
    for q in range(groups):
        cst[step_rows(0, q), :] = zeros

    def cumsum_step(u, cs):
        out = []
        for q in range(groups):
            c = cs[q] + vt[step_rows(u - 1, q), :]
            cst[step_rows(u, q), :] = c
            out.append(c)
        return tuple(out)

    lax.fori_loop(1, tile + halo_v + 1, cumsum_step, (zeros,) * groups, unroll=8)

    low = lax.broadcasted_iota(jnp.int32, (SUBLANES, LANES), 0) < SUBLANES // 2
    w_lo = [POOL_WINDOWS[2 * q] for q in range(groups)]
    w_hi = [POOL_WINDOWS[2 * q + 1] for q in range(groups)]
    w_vec = [jnp.where(low, float(w_lo[q]), float(w_hi[q])) for q in range(groups)]

    def window_step(t, scale):
        u = t + halo_v + 1
        for q in range(groups):
            prev = jnp.where(low, cst[step_rows(u - w_lo[q], q), :],
                             cst[step_rows(u - w_hi[q], q), :])
            win = cst[step_rows(u, q), :] - prev
            zt[step_rows(t, q), :] = scale(win, t, q) - vt[step_rows(t + halo_v, q), :]

    def head_scale(win, t, q):
        pos = (t_idx * tile + t + 1).astype(F32)
        return win / jnp.minimum(pos, w_vec[q])

    def body_scale(win, t, q):
        return win * (1.0 / w_vec[q])

    def head_step(t, carry):
        window_step(t, head_scale)
        return carry

    def body_step(t, carry):
        window_step(t, body_scale)
        return carry

    ramp = max(POOL_WINDOWS)
    lax.fori_loop(0, ramp, head_step, 0, unroll=4)
    lax.fori_loop(ramp, tile, body_step, 0, unroll=8)

    slabs_per_group = n_slabs // len(POOL_WINDOWS)
    for g in range(len(POOL_WINDOWS)):
        z = jnp.concatenate(
            [zt[slab_rows(g * slabs_per_group + s), :] for s in range(slabs_per_group)], axis=1)
        gl = slice(g * slabs_per_group * LANES, (g + 1) * slabs_per_group * LANES)
        y_s[:, gl] = (jnp.dot(z.astype(BF16), wp_ref[g], preferred_element_type=F32)
                      + bp_ref[:, gl])
    o_ref[:, d_rnn:] = _rms_scale(y_s[...], ps_ref[...]).astype(BF16)

    xt[0:halo_x * pitch, :] = xt[tile * pitch:(tile + halo_x) * pitch, :]
    vt[0:halo_v * pitch, :] = vt[tile * pitch:(tile + halo_v) * pitch, :]


def _mixer(proj, cw, cb, wg, ba, bx, lam, beta, wp, bp, ps, *, tile, d_rnn, d_pool):
    b, s, _ = proj.shape
    n_slabs = d_rnn // LANES
    assert d_rnn == d_pool and n_slabs % SUBLANES == 0
    assert n_slabs // len(POOL_WINDOWS) == SUBLANES // 2
    pitch = n_slabs + 1
    halo_x = SUBLANES
    halo_v = max(POOL_WINDOWS)
    assert halo_x >= CONV_WIDTH - 1 and (tile * pitch) % SUBLANES == 0
    vec = lambda width: pl.BlockSpec((1, width), lambda i, j: (0, 0))
    time_major = lambda steps: pltpu.VMEM((steps * pitch, LANES), F32)
    kern = functools.partial(_mixer_kernel, tile=tile, n_slabs=n_slabs, halo_x=halo_x,
                             halo_v=halo_v)
    return pl.pallas_call(
        kern,
        grid=(b, s // tile),
        in_specs=[
            pl.BlockSpec((None, tile, d_rnn), lambda i, j: (i, j, 0)),
            pl.BlockSpec((None, tile, d_rnn), lambda i, j: (i, j, 1)),
            pl.BlockSpec((None, tile, d_pool), lambda i, j: (i, j, 2)),
            pl.BlockSpec((CONV_WIDTH, n_slabs, LANES), lambda i, j: (0, 0, 0)),
            pl.BlockSpec((n_slabs, LANES), lambda i, j: (0, 0)),
            pl.BlockSpec(wg.shape, lambda i, j: (0, 0, 0)),
            vec(d_rnn), vec(d_rnn), vec(d_rnn), vec(d_rnn),
            pl.BlockSpec(wp.shape, lambda i, j: (0, 0, 0)),
            vec(d_pool), vec(d_pool),
        ],
        out_specs=pl.BlockSpec((None, tile, d_rnn + d_pool), lambda i, j: (i, j, 0)),
        out_shape=jax.ShapeDtypeStruct((b, s, d_rnn + d_pool), BF16),
        scratch_shapes=[
            time_major(tile + halo_x),
            time_major(tile),
            time_major(tile),
            time_major(tile),
            pltpu.VMEM((n_slabs, LANES), F32),
            time_major(tile + halo_v),
            time_major(tile + halo_v + 1),
            time_major(tile),
            pltpu.VMEM((tile, d_rnn), F32),
        ],
        compiler_params=_params(2),
        name="mixer",
    )(proj, proj, proj, cw.reshape(CONV_WIDTH, n_slabs, LANES), cb.reshape(n_slabs, LANES),
      wg, ba, bx, lam, beta, wp, bp, ps)


def _matmul_res_kernel(a_ref, w_ref, r_ref, g_ref, o_ref, hb_ref, ss_ref):
    @pl.when(pl.program_id(2) == 0)
    def _():
        o_ref[...] = r_ref[...]

    ss = []
    for cs in _col_chunks(o_ref.shape[1]):
        h = o_ref[:, cs] + jnp.dot(a_ref[...], w_ref[:, cs], preferred_element_type=F32)
        o_ref[:, cs] = h
        hb_ref[:, cs] = (h * g_ref[:, cs]).astype(BF16)
        ss.append(_lane_partial_sumsq(h))
    ss_ref[...] = _tree_sum(ss)


def _matmul_res(a, w, res, g, *, bm, bn, bk, name):
    m, k = a.shape
    n = w.shape[1]
    tile = lambda i, j, kk: (i, j)
    return pl.pallas_call(
        _matmul_res_kernel,
        grid=(m // bm, n // bn, k // bk),
        in_specs=[
            pl.BlockSpec((bm, bk), lambda i, j, kk: (i, kk)),
            pl.BlockSpec((bk, bn), lambda i, j, kk: (kk, j)),
            pl.BlockSpec((bm, bn), tile),
            pl.BlockSpec((1, bn), lambda i, j, kk: (0, j)),
        ],
        out_specs=[pl.BlockSpec((bm, bn), tile), pl.BlockSpec((bm, bn), tile),
                   pl.BlockSpec((bm, LANES), tile)],
        out_shape=[jax.ShapeDtypeStruct((m, n), F32), jax.ShapeDtypeStruct((m, n), BF16),
                   jax.ShapeDtypeStruct((m, (n // bn) * LANES), F32)],
        compiler_params=_params(3),
        name=name,
    )(a, w, res, g)


def _mlp_up_kernel(hb_ref, ss_ref, w_ref, o_ref, rs_ref):
    @pl.when(pl.program_id(1) == 0)
    def _():
        _row_scale(ss_ref, rs_ref, hb_ref.shape[1])

    rs = rs_ref[...]
    for cs in _col_chunks(o_ref.shape[1]):
        z = jnp.dot(hb_ref[...], w_ref[:, cs], preferred_element_type=F32)
        o_ref[:, cs] = jnp.square(jnp.maximum(_scale_rows(z, rs), 0.0)).astype(BF16)


def _mlp_up(hb, ss, w, *, bm, bn):
    m, d = hb.shape
    n = w.shape[1]
    return pl.pallas_call(
        _mlp_up_kernel,
        grid=(m // bm, n // bn),
        in_specs=[
            pl.BlockSpec((bm, d), lambda i, j: (i, 0)),
            pl.BlockSpec((bm, ss.shape[1]), lambda i, j: (i, 0)),
            pl.BlockSpec((d, bn), lambda i, j: (0, j)),
        ],
        out_specs=pl.BlockSpec((bm, bn), lambda i, j: (i, j)),
        out_shape=jax.ShapeDtypeStruct((m, n), BF16),
        scratch_shapes=[pltpu.VMEM((bm, LANES), F32)],
        compiler_params=_params(2),
        name="mlp_up",
    )(hb, ss, w)


def _ple_kernel(hb_ref, ssi_ref, h_ref, wg_ref, p_ref, wp_ref, o_ref, sso_ref, rs_ref):
    @pl.when(pl.program_id(1) == 0)
    def _():
        _row_scale(ssi_ref, rs_ref, hb_ref.shape[1])

    rs = rs_ref[...]
    pb = p_ref[...].astype(BF16)
    ss = []
    for cs in _col_chunks(o_ref.shape[1]):
        z = _scale_rows(jnp.dot(hb_ref[...], wg_ref[:, cs], preferred_element_type=F32), rs)
        gate = 0.5 * jnp.tanh(0.5 * z) + 0.5
        emb = jnp.dot(pb, wp_ref[:, cs], preferred_element_type=F32)
        h = h_ref[:, cs] + gate * emb
        o_ref[:, cs] = h
        ss.append(_lane_partial_sumsq(h))
    sso_ref[...] = _tree_sum(ss)


def _ple(hb, ss, h, wg, p, wp, *, bm, bn):
    m, d = hb.shape
    n = wg.shape[1]
    e = p.shape[1]
    tile = lambda i, j: (i, j)
    return pl.pallas_call(
        _ple_kernel,
        grid=(m // bm, n // bn),
        in_specs=[
            pl.BlockSpec((bm, d), lambda i, j: (i, 0)),
            pl.BlockSpec((bm, ss.shape[1]), lambda i, j: (i, 0)),
            pl.BlockSpec((bm, bn), tile),
            pl.BlockSpec((d, bn), lambda i, j: (0, j)),
            pl.BlockSpec((bm, e), lambda i, j: (i, 0)),
            pl.BlockSpec((e, bn), lambda i, j: (0, j)),
        ],
        out_specs=[pl.BlockSpec((bm, bn), tile), pl.BlockSpec((bm, LANES), tile)],
        out_shape=[jax.ShapeDtypeStruct((m, n), F32),
                   jax.ShapeDtypeStruct((m, (n // bn) * LANES), F32)],
        scratch_shapes=[pltpu.VMEM((bm, LANES), F32)],
        compiler_params=_params(2),
        name="ple",
    )(hb, ss, h, wg, p, wp)


def _final_norm_kernel(h_ref, ss_ref, g_ref, o_ref, rs_ref):
    _row_scale(ss_ref, rs_ref, h_ref.shape[1])
    o_ref[...] = _scale_rows(h_ref[...], rs_ref[...]) * g_ref[...]


def _final_norm(h, ss, g, *, bm):
    m, d = h.shape
    return pl.pallas_call(
        _final_norm_kernel,
        grid=(m // bm,),
        in_specs=[pl.BlockSpec((bm, d), lambda i: (i, 0)),
                  pl.BlockSpec((bm, ss.shape[1]), lambda i: (i, 0)),
                  pl.BlockSpec((1, d), lambda i: (0, 0))],
        out_specs=pl.BlockSpec((bm, d), lambda i: (i, 0)),
        out_shape=jax.ShapeDtypeStruct((m, d), F32),
        scratch_shapes=[pltpu.VMEM((bm, LANES), F32)],
        compiler_params=_params(1),
        name="final_norm",
    )(h, ss, g)


def kernel(x, p, norm_mix_g, w_in, conv_w, conv_b, w_rg_a, b_rg_a, w_rg_x, b_rg_x, lru_lambda,
           beta_rnn, w_pool, b_pool, pool_scale, w_out, norm_mlp_g, w_up, w_down, norm_ple_g,
           w_ple_gate, w_ple_proj, final_norm_g):
    bsz, seq, d_model = x.shape
    assert w_in.shape[0] == 1, "single trunk layer"
    d_rnn = conv_w.shape[-1]
    d_pool = b_pool.shape[-1]
    m = bsz * seq
    row = lambda v: v.reshape(1, -1)
    bf16 = lambda w: w[0].astype(BF16)

    x2 = x.reshape(m, d_model)
    proj = _norm_proj(x2, row(norm_mix_g), bf16(w_in), bm=1024, bn=512)
    w_gates = jnp.concatenate([w_rg_a[0], w_rg_x[0]], axis=-1).astype(BF16)
    mix = _mixer(
        proj.reshape(bsz, seq, -1), conv_w[0], conv_b[0], w_gates, row(b_rg_a), row(b_rg_x),
        row(lru_lambda), row(beta_rnn), bf16(w_pool), row(b_pool), row(pool_scale),
        tile=256, d_rnn=d_rnn, d_pool=d_pool)
    h1, hb1, ss1 = _matmul_res(mix.reshape(m, -1), bf16(w_out), x2, row(norm_mlp_g),
                               bm=1024, bn=1024, bk=d_rnn + d_pool, name="out_proj")
    act = _mlp_up(hb1, ss1, bf16(w_up), bm=1024, bn=1024)
    h2, hb2, ss2 = _matmul_res(act, bf16(w_down), h1, row(norm_ple_g),
                               bm=1024, bn=1024, bk=4096, name="mlp_down")
    h3, ss3 = _ple(hb2, ss2, h2, bf16(w_ple_gate), p.reshape(m, -1), bf16(w_ple_proj),
                   bm=1024, bn=1024)
    out = _final_norm(h3, ss3, row(final_norm_g), bm=256)
    return out.reshape(bsz, seq, d_model)
```

```python
import functools

import jax
import jax.numpy as jnp
from jax import lax
from jax.experimental import pallas as pl
from jax.experimental.pallas import tpu as pltpu

F32 = jnp.float32
BF16 = jnp.bfloat16

EPS = 1e-6
LRU_C = 8.0
CONV_WIDTH = 4
POOL_WINDOWS = (2, 4, 8, 16)
LANES = 128
SUBLANES = 8
BF16_ROWS = 16
VMEM_LIMIT = 60 * 1024 * 1024
COL_CHUNK = 512


def _rms_scale(x, g):
    ms = jnp.mean(x * x, axis=-1, keepdims=True)
    return x * lax.rsqrt(ms + EPS) * g


def _params(n_axes):
    return pltpu.CompilerParams(
        dimension_semantics=("arbitrary",) * n_axes, vmem_limit_bytes=VMEM_LIMIT)


def _lane_tiles(width):
    return [slice(c * LANES, (c + 1) * LANES) for c in range(width // LANES)]


def _col_chunks(width):
    step = min(width, COL_CHUNK)
    return [slice(c, c + step) for c in range(0, width, step)]


def _tree_sum(parts):
    while len(parts) > 1:
        parts = [a + b for a, b in zip(parts[0::2], parts[1::2])] + parts[len(parts) & ~1:]
    return parts[0]


def _lane_partial_sumsq(h):
    return _tree_sum([jnp.square(h[:, cs]) for cs in _lane_tiles(h.shape[1])])


def _row_scale(ss_ref, rs_ref, width):
    ms = jnp.sum(ss_ref[...], axis=-1, keepdims=True) * (1.0 / width)
    rs_ref[...] = jnp.broadcast_to(lax.rsqrt(ms + EPS), rs_ref.shape)


def _scale_rows(z, rs):
    return jnp.concatenate([z[:, cs] * rs for cs in _lane_tiles(z.shape[1])], axis=1)


def _cast_specs(ws, grid):
    steps = grid[0] * grid[1]
    step_block = lambda i, j: (i * grid[1] + j, 0)
    specs, shapes = [], []
    for w in ws:
        rows = w.shape[0] // steps
        assert rows * steps == w.shape[0] and rows % BF16_ROWS == 0
        specs.append(pl.BlockSpec((rows, w.shape[1]), step_block))
        shapes.append(jax.ShapeDtypeStruct(w.shape, BF16))
    return specs, shapes


def _cast_blocks(srcs, dsts):
    for src, dst in zip(srcs, dsts):
        dst[...] = src[...].astype(BF16)


def _norm_proj_kernel(x_ref, g_ref, w_ref, o_ref, xb0, ss0, xb1, ss1, *, chunks):
    i = pl.program_id(0)
    j = pl.program_id(1)
    rc, d = x_ref.shape
    odd = lax.rem(i, 2) == 1
    r0 = pl.multiple_of(jnp.minimum(j, chunks - 1) * rc, rc)

    def prepare(xb_ref, ss_ref):
        for rg in range(0, rc, BF16_ROWS):
            rows = pl.ds(r0 + rg, BF16_ROWS)
            sq = []
            for cs in _lane_tiles(d):
                x = x_ref[rg:rg + BF16_ROWS, cs]
                xb_ref[rows, cs] = (x * g_ref[:, cs]).astype(BF16)
                sq.append(jnp.square(x))
            ss_ref[rows, :] = _tree_sum(sq)

    def step(xb_w, ss_w, xb_r, ss_r):
        prepare(xb_w, ss_w)
        ms = jnp.sum(ss_r[...], axis=-1, keepdims=True) * (1.0 / d)
        rs = jnp.broadcast_to(lax.rsqrt(ms + EPS), ss_r.shape)
        for cs in _col_chunks(o_ref.shape[1]):
            z = jnp.dot(xb_r[...], w_ref[:, cs], preferred_element_type=F32)
            o_ref[:, cs] = _scale_rows(z, rs)

    @pl.when(i == 0)
    def _():
        prepare(xb0, ss0)

    @pl.when(jnp.logical_and(i > 0, odd))
    def _():
        step(xb1, ss1, xb0, ss0)

    @pl.when(jnp.logical_and(i > 0, jnp.logical_not(odd)))
    def _():
        step(xb0, ss0, xb1, ss1)


def _norm_proj(x, g, w, *, bm, bn, rc):
    m, d = x.shape
    n = w.shape[1]
    blocks = m // bm
    chunks = bm // rc
    first = lambda i, j: jnp.where(i > 0, j, 0)
    return pl.pallas_call(
        functools.partial(_norm_proj_kernel, chunks=chunks),
        grid=(blocks + 1, n // bn),
        in_specs=[
            pl.BlockSpec((rc, d), lambda i, j: (jnp.minimum(i, blocks - 1) * chunks
                                                + jnp.minimum(j, chunks - 1), 0)),
            pl.BlockSpec((1, d), lambda i, j: (0, 0)),
            pl.BlockSpec((d, bn), lambda i, j: (0, first(i, j))),
        ],
        out_specs=pl.BlockSpec((bm, bn), lambda i, j: (jnp.maximum(i - 1, 0), first(i, j))),
        out_shape=jax.ShapeDtypeStruct((m, n), F32),
        scratch_shapes=[pltpu.VMEM((bm, d), BF16), pltpu.VMEM((bm, LANES), F32)] * 2,
        compiler_params=_params(2),
        name="norm_in_proj",
    )(x, g, w)


def _mixer_kernel(xr_ref, gr_ref, v_ref, cw_ref, cb_ref, wg_ref, ba_ref, bx_ref, lam_ref,
                  beta_ref, wp_ref, bp_ref, ps_ref, w1_ref, w2_ref, o_ref, w1b_ref, w2b_ref,
                  xt, ct, a_t, b_t, hst, vt, cst, zt, y_s, *, tile, n_slabs, halo_x, halo_v):
    t_idx = pl.program_id(1)
    _cast_blocks((w1_ref, w2_ref), (w1b_ref, w2b_ref))
    d_rnn = n_slabs * LANES
    pitch = n_slabs + 1
    groups = n_slabs // SUBLANES

    def slab_rows(c, t0=0):
        return pl.ds(t0 * pitch + c, tile, stride=pitch)

    def step_rows(step, q):
        return pl.ds(step * pitch + q * SUBLANES, SUBLANES)

    @pl.when(t_idx == 0)
    def _():
        xt[0:halo_x * pitch, :] = jnp.zeros((halo_x * pitch, LANES), F32)
        vt[0:halo_v * pitch, :] = jnp.zeros((halo_v * pitch, LANES), F32)
        hst[...] = jnp.zeros(hst.shape, F32)

    for c in range(n_slabs):
        sl = slice(c * LANES, (c + 1) * LANES)
        xt[slab_rows(c, halo_x), :] = xr_ref[:, sl]
        vt[slab_rows(c, halo_v), :] = v_ref[:, sl]

    cw = [[cw_ref[k, q * SUBLANES:(q + 1) * SUBLANES, :] for q in range(groups)]
          for k in range(CONV_WIDTH)]
    cb = [cb_ref[q * SUBLANES:(q + 1) * SUBLANES, :] for q in range(groups)]

    def conv_step(t, carry):
        for q in range(groups):
            acc = cb[q]
            for k in range(CONV_WIDTH):
                acc = acc + cw[k][q] * xt[step_rows(t + halo_x - (CONV_WIDTH - 1) + k, q), :]
            ct[step_rows(t, q), :] = acc
        return carry

    lax.fori_loop(0, tile, conv_step, 0, unroll=8)

    for c in range(n_slabs):
        sl = slice(c * LANES, (c + 1) * LANES)
        xc = ct[slab_rows(c), :]
        pre = jnp.dot(xc.astype(BF16), wg_ref[c], preferred_element_type=F32)
        th_r = jnp.tanh(0.5 * (pre[:, :LANES] + ba_ref[:, sl]))
        th_i = jnp.tanh(0.5 * (pre[:, LANES:] + bx_ref[:, sl]))
        neg_lam = -lam_ref[:, sl]
        softplus = jnp.maximum(neg_lam, 0.0) + jnp.log1p(jnp.exp(-jnp.abs(neg_lam)))
        log_a = (-0.5 * LRU_C * softplus) * (th_r + 1.0)
        th = jnp.tanh(log_a)
        m2 = -2.0 * th / (1.0 - th)
        mult = jnp.where(m2 > 0.0, m2 * lax.rsqrt(m2), 0.0)
        a_t[slab_rows(c), :] = jnp.exp(log_a)
        b_t[slab_rows(c), :] = mult * (0.5 * (th_i + 1.0)) * xc

    def scan_step(t, hs):
        out = []
        for q in range(groups):
            rows = step_rows(t, q)
            h = a_t[rows, :] * hs[q] + b_t[rows, :]
            b_t[rows, :] = h
            out.append(h)
        return tuple(out)

    h0 = tuple(hst[q * SUBLANES:(q + 1) * SUBLANES, :] for q in range(groups))
    h_last = lax.fori_loop(0, tile, scan_step, h0, unroll=8)
    for q in range(groups):
        hst[q * SUBLANES:(q + 1) * SUBLANES, :] = h_last[q]

    for c in range(n_slabs):
        y_s[:, c * LANES:(c + 1) * LANES] = b_t[slab_rows(c), :]
    y = y_s[...] * jax.nn.gelu(gr_ref[...], approximate=True)
    o_ref[:, 0:d_rnn] = _rms_scale(y, beta_ref[...]).astype(BF16)

    zeros = jnp.zeros((SUBLANES, LANES), F32)
    for q in range(groups):
        cst[step_rows(0, q), :] = zeros

    def cumsum_step(u, cs):
        out = []
        for q in range(groups):
            c = cs[q] + vt[step_rows(u - 1, q), :]
            cst[step_rows(u, q), :] = c
            out.append(c)
        return tuple(out)

    lax.fori_loop(1, tile + halo_v + 1, cumsum_step, (zeros,) * groups, unroll=8)

    low = lax.broadcasted_iota(jnp.int32, (SUBLANES, LANES), 0) < SUBLANES // 2
    w_lo = [POOL_WINDOWS[2 * q] for q in range(groups)]
    w_hi = [POOL_WINDOWS[2 * q + 1] for q in range(groups)]
    w_vec = [jnp.where(low, float(w_lo[q]), float(w_hi[q])) for q in range(groups)]

    def window_step(t, scale):
        u = t + halo_v + 1
        for q in range(groups):
            prev = jnp.where(low, cst[step_rows(u - w_lo[q], q), :],
                             cst[step_rows(u - w_hi[q], q), :])
            win = cst[step_rows(u, q), :] - prev
            zt[step_rows(t, q), :] = scale(win, t, q) - vt[step_rows(t + halo_v, q), :]

    def head_scale(win, t, q):
        pos = (t_idx * tile + t + 1).astype(F32)
        return win / jnp.minimum(pos, w_vec[q])

    def body_scale(win, t, q):
        return win * (1.0 / w_vec[q])

    def head_step(t, carry):
        window_step(t, head_scale)
        return carry

    def body_step(t, carry):
        window_step(t, body_scale)
        return carry

    ramp = max(POOL_WINDOWS)
    lax.fori_loop(0, ramp, head_step, 0, unroll=4)
    lax.fori_loop(ramp, tile, body_step, 0, unroll=8)

    slabs_per_group = n_slabs // len(POOL_WINDOWS)
    for g in range(len(POOL_WINDOWS)):
        z = jnp.concatenate(
            [zt[slab_rows(g * slabs_per_group + s), :] for s in range(slabs_per_group)], axis=1)
        gl = slice(g * slabs_per_group * LANES, (g + 1) * slabs_per_group * LANES)
        y_s[:, gl] = (jnp.dot(z.astype(BF16), wp_ref[g], preferred_element_type=F32)
                      + bp_ref[:, gl])
    o_ref[:, d_rnn:] = _rms_scale(y_s[...], ps_ref[...]).astype(BF16)

    xt[0:halo_x * pitch, :] = xt[tile * pitch:(tile + halo_x) * pitch, :]
    vt[0:halo_v * pitch, :] = vt[tile * pitch:(tile + halo_v) * pitch, :]


def _mixer(proj, cw, cb, wg, ba, bx, lam, beta, wp, bp, ps, casts, *, tile, d_rnn, d_pool):
    b, s, _ = proj.shape
    grid = (b, s // tile)
    cast_specs, cast_shapes = _cast_specs(casts, grid)
    n_slabs = d_rnn // LANES
    assert d_rnn == d_pool and n_slabs % SUBLANES == 0
    assert n_slabs // len(POOL_WINDOWS) == SUBLANES // 2
    pitch = n_slabs + 1
    halo_x = SUBLANES
    halo_v = max(POOL_WINDOWS)
    assert halo_x >= CONV_WIDTH - 1 and (tile * pitch) % SUBLANES == 0
    vec = lambda width: pl.BlockSpec((1, width), lambda i, j: (0, 0))
    time_major = lambda steps: pltpu.VMEM((steps * pitch, LANES), F32)
    kern = functools.partial(_mixer_kernel, tile=tile, n_slabs=n_slabs, halo_x=halo_x,
                             halo_v=halo_v)
    return pl.pallas_call(
        kern,
        grid=grid,
        in_specs=[
            pl.BlockSpec((None, tile, d_rnn), lambda i, j: (i, j, 0)),
            pl.BlockSpec((None, tile, d_rnn), lambda i, j: (i, j, 1)),
            pl.BlockSpec((None, tile, d_pool), lambda i, j: (i, j, 2)),
            pl.BlockSpec((CONV_WIDTH, n_slabs, LANES), lambda i, j: (0, 0, 0)),
            pl.BlockSpec((n_slabs, LANES), lambda i, j: (0, 0)),
            pl.BlockSpec(wg.shape, lambda i, j: (0, 0, 0)),
            vec(d_rnn), vec(d_rnn), vec(d_rnn), vec(d_rnn),
            pl.BlockSpec(wp.shape, lambda i, j: (0, 0, 0)),
            vec(d_pool), vec(d_pool),
        ] + cast_specs,
        out_specs=[pl.BlockSpec((None, tile, d_rnn + d_pool), lambda i, j: (i, j, 0))]
        + cast_specs,
        out_shape=[jax.ShapeDtypeStruct((b, s, d_rnn + d_pool), BF16)] + cast_shapes,
        scratch_shapes=[
            time_major(tile + halo_x),
            time_major(tile),
            time_major(tile),
            time_major(tile),
            pltpu.VMEM((n_slabs, LANES), F32),
            time_major(tile + halo_v),
            time_major(tile + halo_v + 1),
            time_major(tile),
            pltpu.VMEM((tile, d_rnn), F32),
        ],
        compiler_params=_params(2),
        name="mixer",
    )(proj, proj, proj, cw.reshape(CONV_WIDTH, n_slabs, LANES), cb.reshape(n_slabs, LANES),
      wg, ba, bx, lam, beta, wp, bp, ps, *casts)


def _matmul_res_kernel(a_ref, w_ref, r_ref, g_ref, o_ref, hb_ref, ss_ref):
    @pl.when(pl.program_id(2) == 0)
    def _():
        o_ref[...] = r_ref[...]

    ss = []
    for cs in _col_chunks(o_ref.shape[1]):
        h = o_ref[:, cs] + jnp.dot(a_ref[...], w_ref[:, cs], preferred_element_type=F32)
        o_ref[:, cs] = h
        hb_ref[:, cs] = (h * g_ref[:, cs]).astype(BF16)
        ss.append(_lane_partial_sumsq(h))
    ss_ref[...] = _tree_sum(ss)


def _matmul_res(a, w, res, g, *, bm, bn, bk, name):
    m, k = a.shape
    n = w.shape[1]
    tile = lambda i, j, kk: (i, j)
    return pl.pallas_call(
        _matmul_res_kernel,
        grid=(m // bm, n // bn, k // bk),
        in_specs=[
            pl.BlockSpec((bm, bk), lambda i, j, kk: (i, kk)),
            pl.BlockSpec((bk, bn), lambda i, j, kk: (kk, j)),
            pl.BlockSpec((bm, bn), tile),
            pl.BlockSpec((1, bn), lambda i, j, kk: (0, j)),
        ],
        out_specs=[pl.BlockSpec((bm, bn), tile), pl.BlockSpec((bm, bn), tile),
                   pl.BlockSpec((bm, LANES), tile)],
        out_shape=[jax.ShapeDtypeStruct((m, n), F32), jax.ShapeDtypeStruct((m, n), BF16),
                   jax.ShapeDtypeStruct((m, (n // bn) * LANES), F32)],
        compiler_params=_params(3),
        name=name,
    )(a, w, res, g)


def _mlp_up_kernel(hb_ref, ss_ref, w_ref, w1_ref, w2_ref, o_ref, w1b_ref, w2b_ref, rs_ref):
    @pl.when(pl.program_id(1) == 0)
    def _():
        _row_scale(ss_ref, rs_ref, hb_ref.shape[1])

    _cast_blocks((w1_ref, w2_ref), (w1b_ref, w2b_ref))
    rs = rs_ref[...]
    for cs in _col_chunks(o_ref.shape[1]):
        z = jnp.dot(hb_ref[...], w_ref[:, cs], preferred_element_type=F32)
        o_ref[:, cs] = jnp.square(jnp.maximum(_scale_rows(z, rs), 0.0)).astype(BF16)


def _mlp_up(hb, ss, w, casts, *, bm, bn):
    m, d = hb.shape
    n = w.shape[1]
    grid = (m // bm, n // bn)
    cast_specs, cast_shapes = _cast_specs(casts, grid)
    return pl.pallas_call(
        _mlp_up_kernel,
        grid=grid,
        in_specs=[
            pl.BlockSpec((bm, d), lambda i, j: (i, 0)),
            pl.BlockSpec((bm, ss.shape[1]), lambda i, j: (i, 0)),
            pl.BlockSpec((d, bn), lambda i, j: (0, j)),
        ] + cast_specs,
        out_specs=[pl.BlockSpec((bm, bn), lambda i, j: (i, j))] + cast_specs,
        out_shape=[jax.ShapeDtypeStruct((m, n), BF16)] + cast_shapes,
        scratch_shapes=[pltpu.VMEM((bm, LANES), F32)],
        compiler_params=_params(2),
        name="mlp_up",
    )(hb, ss, w, *casts)


def _ple_kernel(hb_ref, ssi_ref, h_ref, wg_ref, p_ref, wp_ref, o_ref, sso_ref, rs_ref):
    @pl.when(pl.program_id(1) == 0)
    def _():
        _row_scale(ssi_ref, rs_ref, hb_ref.shape[1])

    rs = rs_ref[...]
    pb = p_ref[...].astype(BF16)
    ss = []
    for cs in _col_chunks(o_ref.shape[1]):
        z = _scale_rows(jnp.dot(hb_ref[...], wg_ref[:, cs], preferred_element_type=F32), rs)
        gate = 0.5 * jnp.tanh(0.5 * z) + 0.5
        emb = jnp.dot(pb, wp_ref[:, cs], preferred_element_type=F32)
        h = h_ref[:, cs] + gate * emb
        o_ref[:, cs] = h
        ss.append(_lane_partial_sumsq(h))
    sso_ref[...] = _tree_sum(ss)


def _ple(hb, ss, h, wg, p, wp, *, bm, bn):
    m, d = hb.shape
    n = wg.shape[1]
    e = p.shape[1]
    tile = lambda i, j: (i, j)
    return pl.pallas_call(
        _ple_kernel,
        grid=(m // bm, n // bn),
        in_specs=[
            pl.BlockSpec((bm, d), lambda i, j: (i, 0)),
            pl.BlockSpec((bm, ss.shape[1]), lambda i, j: (i, 0)),
            pl.BlockSpec((bm, bn), tile),
            pl.BlockSpec((d, bn), lambda i, j: (0, j)),
            pl.BlockSpec((bm, e), lambda i, j: (i, 0)),
            pl.BlockSpec((e, bn), lambda i, j: (0, j)),
        ],
        out_specs=[pl.BlockSpec((bm, bn), tile), pl.BlockSpec((bm, LANES), tile)],
        out_shape=[jax.ShapeDtypeStruct((m, n), F32),
                   jax.ShapeDtypeStruct((m, (n // bn) * LANES), F32)],
        scratch_shapes=[pltpu.VMEM((bm, LANES), F32)],
        compiler_params=_params(2),
        name="ple",
    )(hb, ss, h, wg, p, wp)


def _final_norm_kernel(h_ref, ss_ref, g_ref, o_ref, rs_ref):
    _row_scale(ss_ref, rs_ref, h_ref.shape[1])
    o_ref[...] = _scale_rows(h_ref[...], rs_ref[...]) * g_ref[...]


def _final_norm(h, ss, g, *, bm):
    m, d = h.shape
    return pl.pallas_call(
        _final_norm_kernel,
        grid=(m // bm,),
        in_specs=[pl.BlockSpec((bm, d), lambda i: (i, 0)),
                  pl.BlockSpec((bm, ss.shape[1]), lambda i: (i, 0)),
                  pl.BlockSpec((1, d), lambda i: (0, 0))],
        out_specs=pl.BlockSpec((bm, d), lambda i: (i, 0)),
        out_shape=jax.ShapeDtypeStruct((m, d), F32),
        scratch_shapes=[pltpu.VMEM((bm, LANES), F32)],
        compiler_params=_params(1),
        name="final_norm",
    )(h, ss, g)


def kernel(x, p, norm_mix_g, w_in, conv_w, conv_b, w_rg_a, b_rg_a, w_rg_x, b_rg_x, lru_lambda,
           beta_rnn, w_pool, b_pool, pool_scale, w_out, norm_mlp_g, w_up, w_down, norm_ple_g,
           w_ple_gate, w_ple_proj, final_norm_g):
    bsz, seq, d_model = x.shape
    assert w_in.shape[0] == 1, "single trunk layer"
    d_rnn = conv_w.shape[-1]
    d_pool = b_pool.shape[-1]
    m = bsz * seq
    row = lambda v: v.reshape(1, -1)
    bf16 = lambda w: w[0].astype(BF16)

    x2 = x.reshape(m, d_model)
    proj = _norm_proj(x2, row(norm_mix_g), bf16(w_in), bm=1024, bn=1024, rc=256)
    w_gates = jnp.concatenate([w_rg_a[0], w_rg_x[0]], axis=-1).astype(BF16)
    mix, w_out_b, w_up_b = _mixer(
        proj.reshape(bsz, seq, -1), conv_w[0], conv_b[0], w_gates, row(b_rg_a), row(b_rg_x),
        row(lru_lambda), row(beta_rnn), bf16(w_pool), row(b_pool), row(pool_scale),
        (w_out[0], w_up[0]), tile=256, d_rnn=d_rnn, d_pool=d_pool)
    h1, hb1, ss1 = _matmul_res(mix.reshape(m, -1), w_out_b, x2, row(norm_mlp_g),
                               bm=1024, bn=1024, bk=d_rnn + d_pool, name="out_proj")
    act, w_down_b, w_gate_b = _mlp_up(hb1, ss1, w_up_b, (w_down[0], w_ple_gate[0]),
                                      bm=1024, bn=1024)
    h2, hb2, ss2 = _matmul_res(act, w_down_b, h1, row(norm_ple_g),
                               bm=1024, bn=1024, bk=4096, name="mlp_down")
    h3, ss3 = _ple(hb2, ss2, h2, w_gate_b, p.reshape(m, -1), bf16(w_ple_proj),
                   bm=1024, bn=1024)
    out = _final_norm(h3, ss3, row(final_norm_g), bm=256)
    return out.reshape(bsz, seq, d_model)
```

```python
import functools

import jax
import jax.numpy as jnp
from jax import lax
from jax.experimental import pallas as pl
from jax.experimental.pallas import tpu as pltpu

F32 = jnp.float32
BF16 = jnp.bfloat16

EPS = 1e-6
LRU_C = 8.0
GELU_C0 = 0.7978845608028654
GELU_C1 = 0.044715
CONV_WIDTH = 4
POOL_WINDOWS = (2, 4, 8, 16)
LANES = 128
SUBLANES = 8
BF16_ROWS = 16
VMEM_LIMIT = 60 * 1024 * 1024
COL_CHUNK = 512


def _rms_scale(x, g):
    ms = jnp.mean(x * x, axis=-1, keepdims=True)
    return x * lax.rsqrt(ms + EPS) * g


def _params(n_axes):
    return pltpu.CompilerParams(
        dimension_semantics=("arbitrary",) * n_axes, vmem_limit_bytes=VMEM_LIMIT)


def _lane_tiles(width):
    return [slice(c * LANES, (c + 1) * LANES) for c in range(width // LANES)]


def _col_chunks(width):
    step = min(width, COL_CHUNK)
    return [slice(c, c + step) for c in range(0, width, step)]


def _tree_sum(parts):
    while len(parts) > 1:
        parts = [a + b for a, b in zip(parts[0::2], parts[1::2])] + parts[len(parts) & ~1:]
    return parts[0]


def _lane_partial_sumsq(h):
    return _tree_sum([jnp.square(h[:, cs]) for cs in _lane_tiles(h.shape[1])])


def _row_scale(ss_ref, rs_ref, width):
    ms = jnp.sum(ss_ref[...], axis=-1, keepdims=True) * (1.0 / width)
    rs_ref[...] = jnp.broadcast_to(lax.rsqrt(ms + EPS), rs_ref.shape)


def _scale_rows(z, rs):
    return jnp.concatenate([z[:, cs] * rs for cs in _lane_tiles(z.shape[1])], axis=1)


def _cast_specs(ws, grid):
    steps = grid[0] * grid[1]
    step_block = lambda i, j: (i * grid[1] + j, 0)
    specs, shapes = [], []
    for w in ws:
        rows = w.shape[0] // steps
        assert rows * steps == w.shape[0] and rows % BF16_ROWS == 0
        specs.append(pl.BlockSpec((rows, w.shape[1]), step_block))
        shapes.append(jax.ShapeDtypeStruct(w.shape, BF16))
    return specs, shapes


def _cast_blocks(srcs, dsts):
    for src, dst in zip(srcs, dsts):
        dst[...] = src[...].astype(BF16)


def _norm_proj_kernel(x_ref, g_ref, w_ref, o_ref, xb0, ss0, xb1, ss1, *, chunks):
    i = pl.program_id(0)
    j = pl.program_id(1)
    rc, d = x_ref.shape
    odd = lax.rem(i, 2) == 1
    r0 = pl.multiple_of(jnp.minimum(j, chunks - 1) * rc, rc)

    def prepare(xb_ref, ss_ref):
        for rg in range(0, rc, BF16_ROWS):
            rows = pl.ds(r0 + rg, BF16_ROWS)
            sq = []
            for cs in _lane_tiles(d):
                x = x_ref[rg:rg + BF16_ROWS, cs]
                xb_ref[rows, cs] = (x * g_ref[:, cs]).astype(BF16)
                sq.append(jnp.square(x))
            ss_ref[rows, :] = _tree_sum(sq)

    def step(xb_w, ss_w, xb_r, ss_r):
        prepare(xb_w, ss_w)
        ms = jnp.sum(ss_r[...], axis=-1, keepdims=True) * (1.0 / d)
        rs = jnp.broadcast_to(lax.rsqrt(ms + EPS), ss_r.shape)
        for cs in _col_chunks(o_ref.shape[1]):
            z = jnp.dot(xb_r[...], w_ref[:, cs], preferred_element_type=F32)
            o_ref[:, cs] = _scale_rows(z, rs)

    @pl.when(i == 0)
    def _():
        prepare(xb0, ss0)

    @pl.when(jnp.logical_and(i > 0, odd))
    def _():
        step(xb1, ss1, xb0, ss0)

    @pl.when(jnp.logical_and(i > 0, jnp.logical_not(odd)))
    def _():
        step(xb0, ss0, xb1, ss1)


def _norm_proj(x, g, w, *, bm, bn, rc):
    m, d = x.shape
    n = w.shape[1]
    blocks = m // bm
    chunks = bm // rc
    first = lambda i, j: jnp.where(i > 0, j, 0)
    return pl.pallas_call(
        functools.partial(_norm_proj_kernel, chunks=chunks),
        grid=(blocks + 1, n // bn),
        in_specs=[
            pl.BlockSpec((rc, d), lambda i, j: (jnp.minimum(i, blocks - 1) * chunks
                                                + jnp.minimum(j, chunks - 1), 0)),
            pl.BlockSpec((1, d), lambda i, j: (0, 0)),
            pl.BlockSpec((d, bn), lambda i, j: (0, first(i, j))),
        ],
        out_specs=pl.BlockSpec((bm, bn), lambda i, j: (jnp.maximum(i - 1, 0), first(i, j))),
        out_shape=jax.ShapeDtypeStruct((m, n), F32),
        scratch_shapes=[pltpu.VMEM((bm, d), BF16), pltpu.VMEM((bm, LANES), F32)] * 2,
        compiler_params=_params(2),
        name="norm_in_proj",
    )(x, g, w)


def _mixer_kernel(xr_ref, gr_ref, v_ref, cw_ref, cb_ref, wg_ref, ba_ref, bx_ref, lam_ref,
                  beta_ref, wp_ref, bp_ref, ps_ref, w1_ref, w2_ref, o_ref, w1b_ref, w2b_ref,
                  xt, ct, a_t, b_t, hst, vt, cst, zt, y_s, *, tile, n_slabs, halo_x, halo_v):
    t_idx = pl.program_id(1)
    _cast_blocks((w1_ref, w2_ref), (w1b_ref, w2b_ref))
    d_rnn = n_slabs * LANES
    pitch = n_slabs + 1
    groups = n_slabs // SUBLANES

    def slab_rows(c, t0=0):
        return pl.ds(t0 * pitch + c, tile, stride=pitch)

    def step_rows(step, q):
        return pl.ds(step * pitch + q * SUBLANES, SUBLANES)

    @pl.when(t_idx == 0)
    def _():
        xt[0:halo_x * pitch, :] = jnp.zeros((halo_x * pitch, LANES), F32)
        vt[0:halo_v * pitch, :] = jnp.zeros((halo_v * pitch, LANES), F32)
        hst[...] = jnp.zeros(hst.shape, F32)

    for c in range(n_slabs):
        sl = slice(c * LANES, (c + 1) * LANES)
        xt[slab_rows(c, halo_x), :] = xr_ref[:, sl]
        vt[slab_rows(c, halo_v), :] = v_ref[:, sl]

    cw = [[cw_ref[k, q * SUBLANES:(q + 1) * SUBLANES, :] for q in range(groups)]
          for k in range(CONV_WIDTH)]
    cb = [cb_ref[q * SUBLANES:(q + 1) * SUBLANES, :] for q in range(groups)]

    def conv_step(t, carry):
        for q in range(groups):
            acc = cb[q]
            for k in range(CONV_WIDTH):
                acc = acc + cw[k][q] * xt[step_rows(t + halo_x - (CONV_WIDTH - 1) + k, q), :]
            ct[step_rows(t, q), :] = acc
        return carry

    lax.fori_loop(0, tile, conv_step, 0, unroll=8)

    for c in range(n_slabs):
        sl = slice(c * LANES, (c + 1) * LANES)
        xc = ct[slab_rows(c), :]
        pre = jnp.dot(xc.astype(BF16), wg_ref[c], preferred_element_type=F32)
        th_r = jnp.tanh(0.5 * (pre[:, :LANES] + ba_ref[:, sl]))
        th_i = jnp.tanh(0.5 * (pre[:, LANES:] + bx_ref[:, sl]))
        neg_lam = -lam_ref[:, sl]
        softplus = jnp.maximum(neg_lam, 0.0) + jnp.log1p(jnp.exp(-jnp.abs(neg_lam)))
        log_a = (-0.5 * LRU_C * softplus) * (th_r + 1.0)
        th = jnp.tanh(log_a)
        q = -0.5 * th / (1.0 - th)
        half_mult = jnp.where(q > 0.0, q * lax.rsqrt(q), 0.0)
        a_t[slab_rows(c), :] = jnp.exp(log_a)
        b_t[slab_rows(c), :] = half_mult * (th_i + 1.0) * xc

    def scan_step(t, hs):
        out = []
        for q in range(groups):
            rows = step_rows(t, q)
            h = a_t[rows, :] * hs[q] + b_t[rows, :]
            b_t[rows, :] = h
            out.append(h)
        return tuple(out)

    h0 = tuple(hst[q * SUBLANES:(q + 1) * SUBLANES, :] for q in range(groups))
    h_last = lax.fori_loop(0, tile, scan_step, h0, unroll=8)
    for q in range(groups):
        hst[q * SUBLANES:(q + 1) * SUBLANES, :] = h_last[q]

    for c in range(n_slabs):
        y_s[:, c * LANES:(c + 1) * LANES] = b_t[slab_rows(c), :]
    g = gr_ref[...]
    inner = g * (GELU_C0 + (GELU_C0 * GELU_C1) * (g * g))
    y = (y_s[...] * g) * (0.5 * jnp.tanh(inner) + 0.5)
    o_ref[:, 0:d_rnn] = _rms_scale(y, beta_ref[...]).astype(BF16)

    zeros = jnp.zeros((SUBLANES, LANES), F32)
    for q in range(groups):
        cst[step_rows(0, q), :] = zeros

    def cumsum_step(u, cs):
        out = []
        for q in range(groups):
            c = cs[q] + vt[step_rows(u - 1, q), :]
            cst[step_rows(u, q), :] = c
            out.append(c)
        return tuple(out)

    lax.fori_loop(1, tile + halo_v + 1, cumsum_step, (zeros,) * groups, unroll=8)

    low = lax.broadcasted_iota(jnp.int32, (SUBLANES, LANES), 0) < SUBLANES // 2
    w_lo = [POOL_WINDOWS[2 * q] for q in range(groups)]
    w_hi = [POOL_WINDOWS[2 * q + 1] for q in range(groups)]
    w_vec = [jnp.where(low, float(w_lo[q]), float(w_hi[q])) for q in range(groups)]

    def window_step(t, scale):
        u = t + halo_v + 1
        for q in range(groups):
            prev = jnp.where(low, cst[step_rows(u - w_lo[q], q), :],
                             cst[step_rows(u - w_hi[q], q), :])
            win = cst[step_rows(u, q), :] - prev
            zt[step_rows(t, q), :] = scale(win, t, q) - vt[step_rows(t + halo_v, q), :]

    def head_scale(win, t, q):
        pos = (t_idx * tile + t + 1).astype(F32)
        return win / jnp.minimum(pos, w_vec[q])

    def body_scale(win, t, q):
        return win * (1.0 / w_vec[q])

    def head_step(t, carry):
        window_step(t, head_scale)
        return carry

    def body_step(t, carry):
        window_step(t, body_scale)
        return carry

    ramp = max(POOL_WINDOWS)
    lax.fori_loop(0, ramp, head_step, 0, unroll=4)
    lax.fori_loop(ramp, tile, body_step, 0, unroll=8)

    slabs_per_group = n_slabs // len(POOL_WINDOWS)
    for g in range(len(POOL_WINDOWS)):
        z = jnp.concatenate(
            [zt[slab_rows(g * slabs_per_group + s), :] for s in range(slabs_per_group)], axis=1)
        gl = slice(g * slabs_per_group * LANES, (g + 1) * slabs_per_group * LANES)
        y_s[:, gl] = (jnp.dot(z.astype(BF16), wp_ref[g], preferred_element_type=F32)
                      + bp_ref[:, gl])
    o_ref[:, d_rnn:] = _rms_scale(y_s[...], ps_ref[...]).astype(BF16)

    xt[0:halo_x * pitch, :] = xt[tile * pitch:(tile + halo_x) * pitch, :]
    vt[0:halo_v * pitch, :] = vt[tile * pitch:(tile + halo_v) * pitch, :]


def _mixer(proj, cw, cb, wg, ba, bx, lam, beta, wp, bp, ps, casts, *, tile, d_rnn, d_pool):
    b, s, _ = proj.shape
    grid = (b, s // tile)
    cast_specs, cast_shapes = _cast_specs(casts, grid)
    n_slabs = d_rnn // LANES
    assert d_rnn == d_pool and n_slabs % SUBLANES == 0
    assert n_slabs // len(POOL_WINDOWS) == SUBLANES // 2
    pitch = n_slabs + 1
    halo_x = SUBLANES
    halo_v = max(POOL_WINDOWS)
    assert halo_x >= CONV_WIDTH - 1 and (tile * pitch) % SUBLANES == 0
    vec = lambda width: pl.BlockSpec((1, width), lambda i, j: (0, 0))
    time_major = lambda steps: pltpu.VMEM((steps * pitch, LANES), F32)
    kern = functools.partial(_mixer_kernel, tile=tile, n_slabs=n_slabs, halo_x=halo_x,
                             halo_v=halo_v)
    return pl.pallas_call(
        kern,
        grid=grid,
        in_specs=[
            pl.BlockSpec((None, tile, d_rnn), lambda i, j: (i, j, 0)),
            pl.BlockSpec((None, tile, d_rnn), lambda i, j: (i, j, 1)),
            pl.BlockSpec((None, tile, d_pool), lambda i, j: (i, j, 2)),
            pl.BlockSpec((CONV_WIDTH, n_slabs, LANES), lambda i, j: (0, 0, 0)),
            pl.BlockSpec((n_slabs, LANES), lambda i, j: (0, 0)),
            pl.BlockSpec(wg.shape, lambda i, j: (0, 0, 0)),
            vec(d_rnn), vec(d_rnn), vec(d_rnn), vec(d_rnn),
            pl.BlockSpec(wp.shape, lambda i, j: (0, 0, 0)),
            vec(d_pool), vec(d_pool),
        ] + cast_specs,
        out_specs=[pl.BlockSpec((None, tile, d_rnn + d_pool), lambda i, j: (i, j, 0))]
        + cast_specs,
        out_shape=[jax.ShapeDtypeStruct((b, s, d_rnn + d_pool), BF16)] + cast_shapes,
        scratch_shapes=[
            time_major(tile + halo_x),
            time_major(tile),
            time_major(tile),
            time_major(tile),
            pltpu.VMEM((n_slabs, LANES), F32),
            time_major(tile + halo_v),
            time_major(tile + halo_v + 1),
            time_major(tile),
            pltpu.VMEM((tile, d_rnn), F32),
        ],
        compiler_params=_params(2),
        name="mixer",
    )(proj, proj, proj, cw.reshape(CONV_WIDTH, n_slabs, LANES), cb.reshape(n_slabs, LANES),
      wg, ba, bx, lam, beta, wp, bp, ps, *casts)


def _matmul_res_kernel(a_ref, w_ref, r_ref, g_ref, o_ref, hb_ref, ss_ref):
    @pl.when(pl.program_id(2) == 0)
    def _():
        o_ref[...] = r_ref[...]

    ss = []
    for cs in _col_chunks(o_ref.shape[1]):
        h = o_ref[:, cs] + jnp.dot(a_ref[...], w_ref[:, cs], preferred_element_type=F32)
        o_ref[:, cs] = h
        hb_ref[:, cs] = (h * g_ref[:, cs]).astype(BF16)
        ss.append(_lane_partial_sumsq(h))
    ss_ref[...] = _tree_sum(ss)


def _matmul_res(a, w, res, g, *, bm, bn, bk, name):
    m, k = a.shape
    n = w.shape[1]
    tile = lambda i, j, kk: (i, j)
    return pl.pallas_call(
        _matmul_res_kernel,
        grid=(m // bm, n // bn, k // bk),
        in_specs=[
            pl.BlockSpec((bm, bk), lambda i, j, kk: (i, kk)),
            pl.BlockSpec((bk, bn), lambda i, j, kk: (kk, j)),
            pl.BlockSpec((bm, bn), tile),
            pl.BlockSpec((1, bn), lambda i, j, kk: (0, j)),
        ],
        out_specs=[pl.BlockSpec((bm, bn), tile), pl.BlockSpec((bm, bn), tile),
                   pl.BlockSpec((bm, LANES), tile)],
        out_shape=[jax.ShapeDtypeStruct((m, n), F32), jax.ShapeDtypeStruct((m, n), BF16),
                   jax.ShapeDtypeStruct((m, (n // bn) * LANES), F32)],
        compiler_params=_params(3),
        name=name,
    )(a, w, res, g)


def _mlp_up_kernel(hb_ref, ss_ref, w_ref, w1_ref, w2_ref, o_ref, w1b_ref, w2b_ref, rs_ref):
    @pl.when(pl.program_id(1) == 0)
    def _():
        _row_scale(ss_ref, rs_ref, hb_ref.shape[1])

    _cast_blocks((w1_ref, w2_ref), (w1b_ref, w2b_ref))
    rs = rs_ref[...]
    for cs in _col_chunks(o_ref.shape[1]):
        z = jnp.dot(hb_ref[...], w_ref[:, cs], preferred_element_type=F32)
        o_ref[:, cs] = jnp.square(jnp.maximum(_scale_rows(z, rs), 0.0)).astype(BF16)


def _mlp_up(hb, ss, w, casts, *, bm, bn):
    m, d = hb.shape
    n = w.shape[1]
    grid = (m // bm, n // bn)
    cast_specs, cast_shapes = _cast_specs(casts, grid)
    return pl.pallas_call(
        _mlp_up_kernel,
        grid=grid,
        in_specs=[
            pl.BlockSpec((bm, d), lambda i, j: (i, 0)),
            pl.BlockSpec((bm, ss.shape[1]), lambda i, j: (i, 0)),
            pl.BlockSpec((d, bn), lambda i, j: (0, j)),
        ] + cast_specs,
        out_specs=[pl.BlockSpec((bm, bn), lambda i, j: (i, j))] + cast_specs,
        out_shape=[jax.ShapeDtypeStruct((m, n), BF16)] + cast_shapes,
        scratch_shapes=[pltpu.VMEM((bm, LANES), F32)],
        compiler_params=_params(2),
        name="mlp_up",
    )(hb, ss, w, *casts)


def _ple_kernel(hb_ref, ssi_ref, h_ref, wg_ref, p_ref, wp_ref, gf_ref, o_ref,
                rs_ref, rsf_ref, ssacc_ref, slab_ref, *, blocks):
    i = pl.program_id(0)
    j = pl.program_id(1)
    d = hb_ref.shape[1]

    @pl.when(jnp.logical_and(i == 0, j == 0))
    def _():
        ssacc_ref[...] = jnp.zeros(ssacc_ref.shape, F32)

    @pl.when(j == 0)
    def _():
        _row_scale(ssi_ref, rs_ref, d)
        _row_scale(ssacc_ref, rsf_ref, d)
        ssacc_ref[...] = jnp.zeros(ssacc_ref.shape, F32)

    def emit_previous():
        o_ref[...] = _scale_rows(slab_ref[j], rsf_ref[...]) * gf_ref[...]

    def compute():
        rs = rs_ref[...]
        pb = p_ref[...].astype(BF16)
        ss = []
        for cs in _col_chunks(o_ref.shape[1]):
            z = _scale_rows(jnp.dot(hb_ref[...], wg_ref[:, cs], preferred_element_type=F32), rs)
            gate = 0.5 * jnp.tanh(0.5 * z) + 0.5
            emb = jnp.dot(pb, wp_ref[:, cs], preferred_element_type=F32)
            h = h_ref[:, cs] + gate * emb
            slab_ref[j, :, cs] = h
            ss.append(_lane_partial_sumsq(h))
        ssacc_ref[...] += _tree_sum(ss)

    @pl.when(i == 0)
    def _():
        compute()

    @pl.when(jnp.logical_and(i > 0, i < blocks))
    def _():
        emit_previous()
        compute()

    @pl.when(i == blocks)
    def _():
        emit_previous()


def _ple(hb, ss, h, wg, p, wp, gf, *, bm, bn):
    m, d = hb.shape
    n = wg.shape[1]
    e = p.shape[1]
    blocks = m // bm
    cur = lambda i: jnp.minimum(i, blocks - 1)
    prev_tile = lambda i, j: (jnp.maximum(i - 1, 0), jnp.where(i > 0, j, 0))
    return pl.pallas_call(
        functools.partial(_ple_kernel, blocks=blocks),
        grid=(blocks + 1, n // bn),
        in_specs=[
            pl.BlockSpec((bm, d), lambda i, j: (cur(i), 0)),
            pl.BlockSpec((bm, ss.shape[1]), lambda i, j: (cur(i), 0)),
            pl.BlockSpec((bm, bn), lambda i, j: (cur(i), j)),
            pl.BlockSpec((d, bn), lambda i, j: (0, j)),
            pl.BlockSpec((bm, e), lambda i, j: (cur(i), 0)),
            pl.BlockSpec((e, bn), lambda i, j: (0, j)),
            pl.BlockSpec((1, bn), lambda i, j: (0, j)),
        ],
        out_specs=pl.BlockSpec((bm, bn), prev_tile),
        out_shape=jax.ShapeDtypeStruct((m, n), F32),
        scratch_shapes=[pltpu.VMEM((bm, LANES), F32), pltpu.VMEM((bm, LANES), F32),
                        pltpu.VMEM((bm, LANES), F32), pltpu.VMEM((n // bn, bm, bn), F32)],
        compiler_params=_params(2),
        name="ple_final_norm",
    )(hb, ss, h, wg, p, wp, gf)


def kernel(x, p, norm_mix_g, w_in, conv_w, conv_b, w_rg_a, b_rg_a, w_rg_x, b_rg_x, lru_lambda,
           beta_rnn, w_pool, b_pool, pool_scale, w_out, norm_mlp_g, w_up, w_down, norm_ple_g,
           w_ple_gate, w_ple_proj, final_norm_g):
    bsz, seq, d_model = x.shape
    assert w_in.shape[0] == 1, "single trunk layer"
    d_rnn = conv_w.shape[-1]
    d_pool = b_pool.shape[-1]
    m = bsz * seq
    row = lambda v: v.reshape(1, -1)
    bf16 = lambda w: w[0].astype(BF16)

    x2 = x.reshape(m, d_model)
    proj = _norm_proj(x2, row(norm_mix_g), bf16(w_in), bm=1024, bn=1024, rc=256)
    w_gates = jnp.concatenate([w_rg_a[0], w_rg_x[0]], axis=-1).astype(BF16)
    mix, w_out_b, w_up_b = _mixer(
        proj.reshape(bsz, seq, -1), conv_w[0], conv_b[0], w_gates, row(b_rg_a), row(b_rg_x),
        row(lru_lambda), row(beta_rnn), bf16(w_pool), row(b_pool), row(pool_scale),
        (w_out[0], w_up[0]), tile=256, d_rnn=d_rnn, d_pool=d_pool)
    h1, hb1, ss1 = _matmul_res(mix.reshape(m, -1), w_out_b, x2, row(norm_mlp_g),
                               bm=1024, bn=1024, bk=d_rnn + d_pool, name="out_proj")
    act, w_down_b, w_gate_b = _mlp_up(hb1, ss1, w_up_b, (w_down[0], w_ple_gate[0]),
                                      bm=1024, bn=1024)
    h2, hb2, ss2 = _matmul_res(act, w_down_b, h1, row(norm_ple_g),
                               bm=1024, bn=1024, bk=4096, name="mlp_down")
    out = _ple(hb2, ss2, h2, w_gate_b, p.reshape(m, -1), bf16(w_ple_proj), row(final_norm_g),
               bm=1024, bn=512)
    return out.reshape(bsz, seq, d_model)
```

```python
import functools

import jax
import jax.numpy as jnp
from jax import lax
from jax.experimental import pallas as pl
from jax.experimental.pallas import tpu as pltpu

F32 = jnp.float32
BF16 = jnp.bfloat16

EPS = 1e-6
LRU_C = 8.0
GELU_C0 = 0.7978845608028654
GELU_C1 = 0.044715
CONV_WIDTH = 4
POOL_WINDOWS = (2, 4, 8, 16)
LANES = 128
SUBLANES = 8
BF16_ROWS = 16
VMEM_LIMIT = 60 * 1024 * 1024
COL_CHUNK = 512


def _rms_scale(x, g):
    ms = jnp.mean(x * x, axis=-1, keepdims=True)
    return x * lax.rsqrt(ms + EPS) * g


def _params(n_axes):
    return pltpu.CompilerParams(
        dimension_semantics=("arbitrary",) * n_axes, vmem_limit_bytes=VMEM_LIMIT)


def _lane_tiles(width):
    return [slice(c * LANES, (c + 1) * LANES) for c in range(width // LANES)]


def _col_chunks(width):
    step = min(width, COL_CHUNK)
    return [slice(c, c + step) for c in range(0, width, step)]


def _tree_sum(parts):
    while len(parts) > 1:
        parts = [a + b for a, b in zip(parts[0::2], parts[1::2])] + parts[len(parts) & ~1:]
    return parts[0]


def _lane_partial_sumsq(h):
    return _tree_sum([jnp.square(h[:, cs]) for cs in _lane_tiles(h.shape[1])])


def _row_scale(ss_ref, rs_ref, width):
    ms = jnp.sum(ss_ref[...], axis=-1, keepdims=True) * (1.0 / width)
    rs_ref[...] = jnp.broadcast_to(lax.rsqrt(ms + EPS), rs_ref.shape)


def _scale_rows(z, rs):
    return jnp.concatenate([z[:, cs] * rs for cs in _lane_tiles(z.shape[1])], axis=1)


def _cast_specs(ws, grid):
    steps = grid[0] * grid[1]
    step_block = lambda i, j: (i * grid[1] + j, 0)
    specs, shapes = [], []
    for w in ws:
        rows = w.shape[0] // steps
        assert rows * steps == w.shape[0] and rows % BF16_ROWS == 0
        specs.append(pl.BlockSpec((rows, w.shape[1]), step_block))
        shapes.append(jax.ShapeDtypeStruct(w.shape, BF16))
    return specs, shapes


def _cast_blocks(srcs, dsts):
    for src, dst in zip(srcs, dsts):
        dst[...] = src[...].astype(BF16)


def _norm_proj_kernel(x_ref, g_ref, w_ref, o_ref, xb0, ss0, xb1, ss1, *, chunks):
    i = pl.program_id(0)
    j = pl.program_id(1)
    rc, d = x_ref.shape
    odd = lax.rem(i, 2) == 1
    r0 = pl.multiple_of(jnp.minimum(j, chunks - 1) * rc, rc)

    def prepare(xb_ref, ss_ref):
        for rg in range(0, rc, BF16_ROWS):
            rows = pl.ds(r0 + rg, BF16_ROWS)
            sq = []
            for cs in _lane_tiles(d):
                x = x_ref[rg:rg + BF16_ROWS, cs]
                xb_ref[rows, cs] = (x * g_ref[:, cs]).astype(BF16)
                sq.append(jnp.square(x))
            ss_ref[rows, :] = _tree_sum(sq)

    def step(xb_w, ss_w, xb_r, ss_r):
        prepare(xb_w, ss_w)
        ms = jnp.sum(ss_r[...], axis=-1, keepdims=True) * (1.0 / d)
        rs = jnp.broadcast_to(lax.rsqrt(ms + EPS), ss_r.shape)
        for cs in _col_chunks(o_ref.shape[1]):
            z = jnp.dot(xb_r[...], w_ref[:, cs], preferred_element_type=F32)
            o_ref[:, cs] = _scale_rows(z, rs)

    @pl.when(i == 0)
    def _():
        prepare(xb0, ss0)

    @pl.when(jnp.logical_and(i > 0, odd))
    def _():
        step(xb1, ss1, xb0, ss0)

    @pl.when(jnp.logical_and(i > 0, jnp.logical_not(odd)))
    def _():
        step(xb0, ss0, xb1, ss1)


def _norm_proj(x, g, w, *, bm, bn, rc):
    m, d = x.shape
    n = w.shape[1]
    blocks = m // bm
    chunks = bm // rc
    first = lambda i, j: jnp.where(i > 0, j, 0)
    return pl.pallas_call(
        functools.partial(_norm_proj_kernel, chunks=chunks),
        grid=(blocks + 1, n // bn),
        in_specs=[
            pl.BlockSpec((rc, d), lambda i, j: (jnp.minimum(i, blocks - 1) * chunks
                                                + jnp.minimum(j, chunks - 1), 0)),
            pl.BlockSpec((1, d), lambda i, j: (0, 0)),
            pl.BlockSpec((d, bn), lambda i, j: (0, first(i, j))),
        ],
        out_specs=pl.BlockSpec((bm, bn), lambda i, j: (jnp.maximum(i - 1, 0), first(i, j))),
        out_shape=jax.ShapeDtypeStruct((m, n), F32),
        scratch_shapes=[pltpu.VMEM((bm, d), BF16), pltpu.VMEM((bm, LANES), F32)] * 2,
        compiler_params=_params(2),
        name="norm_in_proj",
    )(x, g, w)


def _mixer_kernel(xr_ref, gr_ref, v_ref, cw_ref, cb_ref, wg_ref, ba_ref, bx_ref, lam_ref,
                  beta_ref, wp_ref, bp_ref, ps_ref, w1_ref, w2_ref, o_ref, w1b_ref, w2b_ref,
                  xt, ct, a_t, b_t, hst, vt, zt, y_s, *, tile, n_slabs, halo_x, halo_v):
    t_idx = pl.program_id(1)
    _cast_blocks((w1_ref, w2_ref), (w1b_ref, w2b_ref))
    d_rnn = n_slabs * LANES
    pitch = n_slabs + 1
    groups = n_slabs // SUBLANES

    def slab_rows(c, t0=0):
        return pl.ds(t0 * pitch + c, tile, stride=pitch)

    def step_rows(step, q):
        return pl.ds(step * pitch + q * SUBLANES, SUBLANES)

    @pl.when(t_idx == 0)
    def _():
        xt[0:halo_x * pitch, :] = jnp.zeros((halo_x * pitch, LANES), F32)
        vt[0:halo_v * pitch, :] = jnp.zeros((halo_v * pitch, LANES), F32)
        hst[...] = jnp.zeros(hst.shape, F32)

    for c in range(n_slabs):
        sl = slice(c * LANES, (c + 1) * LANES)
        xt[slab_rows(c, halo_x), :] = xr_ref[:, sl]
        vt[slab_rows(c, halo_v), :] = v_ref[:, sl]

    cw = [[cw_ref[k, q * SUBLANES:(q + 1) * SUBLANES, :] for q in range(groups)]
          for k in range(CONV_WIDTH)]
    cb = [cb_ref[q * SUBLANES:(q + 1) * SUBLANES, :] for q in range(groups)]

    def conv_step(t, carry):
        for q in range(groups):
            acc = cb[q]
            for k in range(CONV_WIDTH):
                acc = acc + cw[k][q] * xt[step_rows(t + halo_x - (CONV_WIDTH - 1) + k, q), :]
            ct[step_rows(t, q), :] = acc
        return carry

    for t in range(tile):
        conv_step(t, 0)

    for c in range(n_slabs):
        sl = slice(c * LANES, (c + 1) * LANES)
        xc = ct[slab_rows(c), :]
        pre = jnp.dot(xc.astype(BF16), wg_ref[c], preferred_element_type=F32)
        th_r = jnp.tanh(0.5 * (pre[:, :LANES] + ba_ref[:, sl]))
        th_i = jnp.tanh(0.5 * (pre[:, LANES:] + bx_ref[:, sl]))
        neg_lam = -lam_ref[:, sl]
        softplus = jnp.maximum(neg_lam, 0.0) + jnp.log1p(jnp.exp(-jnp.abs(neg_lam)))
        log_a = (-0.5 * LRU_C * softplus) * (th_r + 1.0)
        th = jnp.tanh(log_a)
        n = -0.5 * th
        half_mult = jnp.where(n > 0.0, n * lax.rsqrt(n * (1.0 - th)), 0.0)
        a_t[slab_rows(c), :] = jnp.exp(log_a)
        b_t[slab_rows(c), :] = half_mult * (th_i + 1.0) * xc

    def scan_step(t, hs):
        out = []
        for q in range(groups):
            rows = step_rows(t, q)
            h = a_t[rows, :] * hs[q] + b_t[rows, :]
            b_t[rows, :] = h
            out.append(h)
        return tuple(out)

    h0 = tuple(hst[q * SUBLANES:(q + 1) * SUBLANES, :] for q in range(groups))
    h_last = h0
    for t in range(tile):
        h_last = scan_step(t, h_last)
    for q in range(groups):
        hst[q * SUBLANES:(q + 1) * SUBLANES, :] = h_last[q]

    for c in range(n_slabs):
        y_s[:, c * LANES:(c + 1) * LANES] = b_t[slab_rows(c), :]
    g = gr_ref[...]
    inner = g * (GELU_C0 + (GELU_C0 * GELU_C1) * (g * g))
    y = (y_s[...] * g) * (0.5 * jnp.tanh(inner) + 0.5)
    o_ref[:, 0:d_rnn] = _rms_scale(y, beta_ref[...]).astype(BF16)

    low = lax.broadcasted_iota(jnp.int32, (SUBLANES, LANES), 0) < SUBLANES // 2
    w_lo = [POOL_WINDOWS[2 * q] for q in range(groups)]
    w_hi = [POOL_WINDOWS[2 * q + 1] for q in range(groups)]
    w_vec = [jnp.where(low, float(w_lo[q]), float(w_hi[q])) for q in range(groups)]
    ramp = max(POOL_WINDOWS)

    zeros = jnp.zeros((SUBLANES, LANES), F32)
    cs = [[zeros] * groups]
    for u in range(1, tile + halo_v + 1):
        frame = [vt[step_rows(u - 1, q), :] for q in range(groups)]
        cs.append([cs[u - 1][q] + frame[q] for q in range(groups)])
        t = u - halo_v - 1
        if t < 0:
            continue
        for q in range(groups):
            win = cs[u][q] - jnp.where(low, cs[u - w_lo[q]][q], cs[u - w_hi[q]][q])
            if t < ramp:
                pos = (t_idx * tile + t + 1).astype(F32)
                mean = win / jnp.minimum(pos, w_vec[q])
            else:
                mean = win * (1.0 / w_vec[q])
            zt[step_rows(t, q), :] = mean - frame[q]

    slabs_per_group = n_slabs // len(POOL_WINDOWS)
    for g in range(len(POOL_WINDOWS)):
        z = jnp.concatenate(
            [zt[slab_rows(g * slabs_per_group + s), :] for s in range(slabs_per_group)], axis=1)
        gl = slice(g * slabs_per_group * LANES, (g + 1) * slabs_per_group * LANES)
        y_s[:, gl] = (jnp.dot(z.astype(BF16), wp_ref[g], preferred_element_type=F32)
                      + bp_ref[:, gl])
    o_ref[:, d_rnn:] = _rms_scale(y_s[...], ps_ref[...]).astype(BF16)

    xt[0:halo_x * pitch, :] = xt[tile * pitch:(tile + halo_x) * pitch, :]
    vt[0:halo_v * pitch, :] = vt[tile * pitch:(tile + halo_v) * pitch, :]


def _mixer(proj, cw, cb, wg, ba, bx, lam, beta, wp, bp, ps, casts, *, tile, d_rnn, d_pool):
    b, s, _ = proj.shape
    grid = (b, s // tile)
    cast_specs, cast_shapes = _cast_specs(casts, grid)
    n_slabs = d_rnn // LANES
    assert d_rnn == d_pool and n_slabs % SUBLANES == 0
    assert n_slabs // len(POOL_WINDOWS) == SUBLANES // 2
    pitch = n_slabs + 1
    halo_x = SUBLANES
    halo_v = max(POOL_WINDOWS)
    assert halo_x >= CONV_WIDTH - 1 and (tile * pitch) % SUBLANES == 0
    vec = lambda width: pl.BlockSpec((1, width), lambda i, j: (0, 0))
    time_major = lambda steps: pltpu.VMEM((steps * pitch, LANES), F32)
    kern = functools.partial(_mixer_kernel, tile=tile, n_slabs=n_slabs, halo_x=halo_x,
                             halo_v=halo_v)
    return pl.pallas_call(
        kern,
        grid=grid,
        in_specs=[
            pl.BlockSpec((None, tile, d_rnn), lambda i, j: (i, j, 0)),
            pl.BlockSpec((None, tile, d_rnn), lambda i, j: (i, j, 1)),
            pl.BlockSpec((None, tile, d_pool), lambda i, j: (i, j, 2)),
            pl.BlockSpec((CONV_WIDTH, n_slabs, LANES), lambda i, j: (0, 0, 0)),
            pl.BlockSpec((n_slabs, LANES), lambda i, j: (0, 0)),
            pl.BlockSpec(wg.shape, lambda i, j: (0, 0, 0)),
            vec(d_rnn), vec(d_rnn), vec(d_rnn), vec(d_rnn),
            pl.BlockSpec(wp.shape, lambda i, j: (0, 0, 0)),
            vec(d_pool), vec(d_pool),
        ] + cast_specs,
        out_specs=[pl.BlockSpec((None, tile, d_rnn + d_pool), lambda i, j: (i, j, 0))]
        + cast_specs,
        out_shape=[jax.ShapeDtypeStruct((b, s, d_rnn + d_pool), BF16)] + cast_shapes,
        scratch_shapes=[
            time_major(tile + halo_x),
            time_major(tile),
            time_major(tile),
            time_major(tile),
            pltpu.VMEM((n_slabs, LANES), F32),
            time_major(tile + halo_v),
            time_major(tile),
            pltpu.VMEM((tile, d_rnn), F32),
        ],
        compiler_params=_params(2),
        name="mixer",
    )(proj, proj, proj, cw.reshape(CONV_WIDTH, n_slabs, LANES), cb.reshape(n_slabs, LANES),
      wg, ba, bx, lam, beta, wp, bp, ps, *casts)


def _matmul_res_kernel(a_ref, w_ref, r_ref, g_ref, o_ref, hb_ref, ss_ref):
    @pl.when(pl.program_id(2) == 0)
    def _():
        o_ref[...] = r_ref[...]

    ss = []
    for cs in _col_chunks(o_ref.shape[1]):
        h = o_ref[:, cs] + jnp.dot(a_ref[...], w_ref[:, cs], preferred_element_type=F32)
        o_ref[:, cs] = h
        hb_ref[:, cs] = (h * g_ref[:, cs]).astype(BF16)
        ss.append(_lane_partial_sumsq(h))
    ss_ref[...] = _tree_sum(ss)


def _matmul_res(a, w, res, g, *, bm, bn, bk, name):
    m, k = a.shape
    n = w.shape[1]
    tile = lambda i, j, kk: (i, j)
    return pl.pallas_call(
        _matmul_res_kernel,
        grid=(m // bm, n // bn, k // bk),
        in_specs=[
            pl.BlockSpec((bm, bk), lambda i, j, kk: (i, kk)),
            pl.BlockSpec((bk, bn), lambda i, j, kk: (kk, j)),
            pl.BlockSpec((bm, bn), tile),
            pl.BlockSpec((1, bn), lambda i, j, kk: (0, j)),
        ],
        out_specs=[pl.BlockSpec((bm, bn), tile), pl.BlockSpec((bm, bn), tile),
                   pl.BlockSpec((bm, LANES), tile)],
        out_shape=[jax.ShapeDtypeStruct((m, n), F32), jax.ShapeDtypeStruct((m, n), BF16),
                   jax.ShapeDtypeStruct((m, (n // bn) * LANES), F32)],
        compiler_params=_params(3),
        name=name,
    )(a, w, res, g)


def _mlp_up_kernel(hb_ref, ss_ref, w_ref, w1_ref, w2_ref, o_ref, w1b_ref, w2b_ref, rs_ref):
    @pl.when(pl.program_id(1) == 0)
    def _():
        _row_scale(ss_ref, rs_ref, hb_ref.shape[1])

    _cast_blocks((w1_ref, w2_ref), (w1b_ref, w2b_ref))
    rs = rs_ref[...]
    for cs in _col_chunks(o_ref.shape[1]):
        z = jnp.dot(hb_ref[...], w_ref[:, cs], preferred_element_type=F32)
        o_ref[:, cs] = jnp.square(jnp.maximum(_scale_rows(z, rs), 0.0)).astype(BF16)


def _mlp_up(hb, ss, w, casts, *, bm, bn):
    m, d = hb.shape
    n = w.shape[1]
    grid = (m // bm, n // bn)
    cast_specs, cast_shapes = _cast_specs(casts, grid)
    return pl.pallas_call(
        _mlp_up_kernel,
        grid=grid,
        in_specs=[
            pl.BlockSpec((bm, d), lambda i, j: (i, 0)),
            pl.BlockSpec((bm, ss.shape[1]), lambda i, j: (i, 0)),
            pl.BlockSpec((d, bn), lambda i, j: (0, j)),
        ] + cast_specs,
        out_specs=[pl.BlockSpec((bm, bn), lambda i, j: (i, j))] + cast_specs,
        out_shape=[jax.ShapeDtypeStruct((m, n), BF16)] + cast_shapes,
        scratch_shapes=[pltpu.VMEM((bm, LANES), F32)],
        compiler_params=_params(2),
        name="mlp_up",
    )(hb, ss, w, *casts)


def _ple_kernel(hb_ref, ssi_ref, h_ref, wg_ref, p_ref, wp_ref, gf_ref, o_ref,
                rs_ref, rsf_ref, ssacc_ref, slab_ref, *, blocks):
    i = pl.program_id(0)
    j = pl.program_id(1)
    d = hb_ref.shape[1]

    @pl.when(jnp.logical_and(i == 0, j == 0))
    def _():
        ssacc_ref[...] = jnp.zeros(ssacc_ref.shape, F32)

    @pl.when(j == 0)
    def _():
        _row_scale(ssi_ref, rs_ref, d)
        _row_scale(ssacc_ref, rsf_ref, d)
        ssacc_ref[...] = jnp.zeros(ssacc_ref.shape, F32)

    def emit_previous():
        o_ref[...] = _scale_rows(slab_ref[j], rsf_ref[...]) * gf_ref[...]

    def compute():
        rs = rs_ref[...]
        pb = p_ref[...].astype(BF16)
        ss = []
        for cs in _col_chunks(o_ref.shape[1]):
            z = _scale_rows(jnp.dot(hb_ref[...], wg_ref[:, cs], preferred_element_type=F32), rs)
            gate = 0.5 * jnp.tanh(0.5 * z) + 0.5
            emb = jnp.dot(pb, wp_ref[:, cs], preferred_element_type=F32)
            h = h_ref[:, cs] + gate * emb
            slab_ref[j, :, cs] = h
            ss.append(_lane_partial_sumsq(h))
        ssacc_ref[...] += _tree_sum(ss)

    @pl.when(i == 0)
    def _():
        compute()

    @pl.when(jnp.logical_and(i > 0, i < blocks))
    def _():
        emit_previous()
        compute()

    @pl.when(i == blocks)
    def _():
        emit_previous()


def _ple(hb, ss, h, wg, p, wp, gf, *, bm, bn):
    m, d = hb.shape
    n = wg.shape[1]
    e = p.shape[1]
    blocks = m // bm
    cur = lambda i: jnp.minimum(i, blocks - 1)
    prev_tile = lambda i, j: (jnp.maximum(i - 1, 0), jnp.where(i > 0, j, 0))
    return pl.pallas_call(
        functools.partial(_ple_kernel, blocks=blocks),
        grid=(blocks + 1, n // bn),
        in_specs=[
            pl.BlockSpec((bm, d), lambda i, j: (cur(i), 0)),
            pl.BlockSpec((bm, ss.shape[1]), lambda i, j: (cur(i), 0)),
            pl.BlockSpec((bm, bn), lambda i, j: (cur(i), j)),
            pl.BlockSpec((d, bn), lambda i, j: (0, j)),
            pl.BlockSpec((bm, e), lambda i, j: (cur(i), 0)),
            pl.BlockSpec((e, bn), lambda i, j: (0, j)),
            pl.BlockSpec((1, bn), lambda i, j: (0, j)),
        ],
        out_specs=pl.BlockSpec((bm, bn), prev_tile),
        out_shape=jax.ShapeDtypeStruct((m, n), F32),
        scratch_shapes=[pltpu.VMEM((bm, LANES), F32), pltpu.VMEM((bm, LANES), F32),
                        pltpu.VMEM((bm, LANES), F32), pltpu.VMEM((n // bn, bm, bn), F32)],
        compiler_params=_params(2),
        name="ple_final_norm",
    )(hb, ss, h, wg, p, wp, gf)


def kernel(x, p, norm_mix_g, w_in, conv_w, conv_b, w_rg_a, b_rg_a, w_rg_x, b_rg_x, lru_lambda,
           beta_rnn, w_pool, b_pool, pool_scale, w_out, norm_mlp_g, w_up, w_down, norm_ple_g,
           w_ple_gate, w_ple_proj, final_norm_g):
    bsz, seq, d_model = x.shape
    assert w_in.shape[0] == 1, "single trunk layer"
    d_rnn = conv_w.shape[-1]
    d_pool = b_pool.shape[-1]
    m = bsz * seq
    row = lambda v: v.reshape(1, -1)
    bf16 = lambda w: w[0].astype(BF16)

    x2 = x.reshape(m, d_model)
    proj = _norm_proj(x2, row(norm_mix_g), bf16(w_in), bm=1024, bn=1024, rc=256)
    w_gates = jnp.concatenate([w_rg_a[0], w_rg_x[0]], axis=-1).astype(BF16)
    mix, w_out_b, w_up_b = _mixer(
        proj.reshape(bsz, seq, -1), conv_w[0], conv_b[0], w_gates, row(b_rg_a), row(b_rg_x),
        row(lru_lambda), row(beta_rnn), bf16(w_pool), row(b_pool), row(pool_scale),
        (w_out[0], w_up[0]), tile=256, d_rnn=d_rnn, d_pool=d_pool)
    h1, hb1, ss1 = _matmul_res(mix.reshape(m, -1), w_out_b, x2, row(norm_mlp_g),
                               bm=1024, bn=1024, bk=d_rnn + d_pool, name="out_proj")
    act, w_down_b, w_gate_b = _mlp_up(hb1, ss1, w_up_b, (w_down[0], w_ple_gate[0]),
                                      bm=1024, bn=1024)
    h2, hb2, ss2 = _matmul_res(act, w_down_b, h1, row(norm_ple_g),
                               bm=1024, bn=1024, bk=4096, name="mlp_down")
    out = _ple(hb2, ss2, h2, w_gate_b, p.reshape(m, -1), bf16(w_ple_proj), row(final_norm_g),
               bm=1024, bn=512)
    return out.reshape(bsz, seq, d_model)
```

```python
import functools

import jax
import jax.numpy as jnp
from jax import lax
from jax.experimental import pallas as pl
from jax.experimental.pallas import tpu as pltpu

F32 = jnp.float32
BF16 = jnp.bfloat16

EPS = 1e-6
LRU_C = 8.0
GELU_C0 = 0.7978845608028654
GELU_C1 = 0.044715
CONV_WIDTH = 4
POOL_WINDOWS = (2, 4, 8, 16)
LANES = 128
SUBLANES = 8
BF16_ROWS = 16
VMEM_LIMIT = 60 * 1024 * 1024
COL_CHUNK = 512

ROW_BLOCK = 1024
COL_TILE = 1024
DOWN_K_TILE = 4096
PLE_COL_TILE = 512
PREP_ROWS = 256
MIXER_TILE = 256


def _rms_scale(x, g):
    ms = jnp.mean(x * x, axis=-1, keepdims=True)
    return x * lax.rsqrt(ms + EPS) * g


def _params(n_axes):
    return pltpu.CompilerParams(
        dimension_semantics=("arbitrary",) * n_axes, vmem_limit_bytes=VMEM_LIMIT)


def _lane_tiles(width):
    return [slice(c * LANES, (c + 1) * LANES) for c in range(width // LANES)]


def _col_chunks(width):
    step = min(width, COL_CHUNK)
    return [slice(c, c + step) for c in range(0, width, step)]


def _tree_sum(parts):
    while len(parts) > 1:
        parts = [a + b for a, b in zip(parts[0::2], parts[1::2])] + parts[len(parts) & ~1:]
    return parts[0]


def _lane_partial_sumsq(h):
    return _tree_sum([jnp.square(h[:, cs]) for cs in _lane_tiles(h.shape[1])])


def _row_scale(ss_ref, rs_ref, width):
    ms = jnp.sum(ss_ref[...], axis=-1, keepdims=True) * (1.0 / width)
    rs_ref[...] = jnp.broadcast_to(lax.rsqrt(ms + EPS), rs_ref.shape)


def _scale_rows(z, rs):
    return jnp.concatenate([z[:, cs] * rs for cs in _lane_tiles(z.shape[1])], axis=1)


def _cast_specs(ws, steps, step_of):
    step_block = lambda i, j: (step_of(i, j), 0)
    specs, shapes = [], []
    for w in ws:
        rows = w.shape[0] // steps
        assert rows * steps == w.shape[0] and rows % BF16_ROWS == 0
        specs.append(pl.BlockSpec((rows, w.shape[1]), step_block))
        shapes.append(jax.ShapeDtypeStruct(w.shape, BF16))
    return specs, shapes


def _cast_blocks(srcs, dsts):
    for src, dst in zip(srcs, dsts):
        dst[...] = src[...].astype(BF16)


def _norm_proj_kernel(x_ref, g_ref, w_ref, w1_ref, o_ref, w1b_ref, xb0, ss0, xb1, ss1, *, chunks):
    i = pl.program_id(0)
    j = pl.program_id(1)
    rc, d = x_ref.shape
    odd = lax.rem(i, 2) == 1
    r0 = pl.multiple_of(jnp.minimum(j, chunks - 1) * rc, rc)

    def prepare(xb_ref, ss_ref):
        _cast_blocks((w1_ref,), (w1b_ref,))
        for rg in range(0, rc, BF16_ROWS):
            rows = pl.ds(r0 + rg, BF16_ROWS)
            sq = []
            for cs in _lane_tiles(d):
                x = x_ref[rg:rg + BF16_ROWS, cs]
                xb_ref[rows, cs] = (x * g_ref[:, cs]).astype(BF16)
                sq.append(jnp.square(x))
            ss_ref[rows, :] = _tree_sum(sq)

    def step(xb_w, ss_w, xb_r, ss_r):
        prepare(xb_w, ss_w)
        ms = jnp.sum(ss_r[...], axis=-1, keepdims=True) * (1.0 / d)
        rs = jnp.broadcast_to(lax.rsqrt(ms + EPS), ss_r.shape)
        for cs in _col_chunks(o_ref.shape[1]):
            z = jnp.dot(xb_r[...], w_ref[:, cs], preferred_element_type=F32)
            o_ref[:, cs] = _scale_rows(z, rs)

    @pl.when(i == 0)
    def _():
        prepare(xb0, ss0)

    @pl.when(jnp.logical_and(i > 0, odd))
    def _():
        step(xb1, ss1, xb0, ss0)

    @pl.when(jnp.logical_and(i > 0, jnp.logical_not(odd)))
    def _():
        step(xb0, ss0, xb1, ss1)


def _norm_proj(x, g, w, casts, *, bm, bn, rc):
    m, d = x.shape
    n = w.shape[1]
    blocks = m // bm
    chunks = bm // rc
    assert chunks <= n // bn
    first = lambda i, j: jnp.where(i > 0, j, 0)
    cast_specs, cast_shapes = _cast_specs(
        casts, blocks * chunks,
        lambda i, j: jnp.where(i > 0, (i - 1) * chunks + jnp.minimum(j, chunks - 1), 0))
    return pl.pallas_call(
        functools.partial(_norm_proj_kernel, chunks=chunks),
        grid=(blocks + 1, n // bn),
        in_specs=[
            pl.BlockSpec((rc, d), lambda i, j: (jnp.minimum(i, blocks - 1) * chunks
                                                + jnp.minimum(j, chunks - 1), 0)),
            pl.BlockSpec((1, d), lambda i, j: (0, 0)),
            pl.BlockSpec((d, bn), lambda i, j: (0, first(i, j))),
        ] + cast_specs,
        out_specs=[pl.BlockSpec((bm, bn), lambda i, j: (jnp.maximum(i - 1, 0), first(i, j)))]
        + cast_specs,
        out_shape=[jax.ShapeDtypeStruct((m, n), F32)] + cast_shapes,
        scratch_shapes=[pltpu.VMEM((bm, d), BF16), pltpu.VMEM((bm, LANES), F32)] * 2,
        compiler_params=_params(2),
        name="norm_in_proj",
    )(x, g, w, *casts)


def _mixer_kernel(xr_ref, gr_ref, v_ref, cw_ref, cb_ref, wg_ref, ba_ref, bx_ref, lam_ref,
                  beta_ref, wp_ref, bp_ref, ps_ref, w1_ref, o_ref, w1b_ref,
                  xt, ct, a_t, b_t, hst, vt, zt, y_s, *, tile, n_slabs, halo_x, halo_v):
    t_idx = pl.program_id(1)
    _cast_blocks((w1_ref,), (w1b_ref,))
    d_rnn = n_slabs * LANES
    pitch = n_slabs + 1
    groups = n_slabs // SUBLANES

    def slab_rows(c, t0=0):
        return pl.ds(t0 * pitch + c, tile, stride=pitch)

    def step_rows(step, q):
        return pl.ds(step * pitch + q * SUBLANES, SUBLANES)

    @pl.when(t_idx == 0)
    def _():
        xt[0:halo_x * pitch, :] = jnp.zeros((halo_x * pitch, LANES), F32)
        vt[0:halo_v * pitch, :] = jnp.zeros((halo_v * pitch, LANES), F32)
        hst[...] = jnp.zeros(hst.shape, F32)

    for c in range(n_slabs):
        sl = slice(c * LANES, (c + 1) * LANES)
        xt[slab_rows(c, halo_x), :] = xr_ref[:, sl]
        vt[slab_rows(c, halo_v), :] = v_ref[:, sl]

    cw = [[cw_ref[k, q * SUBLANES:(q + 1) * SUBLANES, :] for q in range(groups)]
          for k in range(CONV_WIDTH)]
    cb = [cb_ref[q * SUBLANES:(q + 1) * SUBLANES, :] for q in range(groups)]

    def conv_step(t, carry):
        for q in range(groups):
            acc = cb[q]
            for k in range(CONV_WIDTH):
                acc = acc + cw[k][q] * xt[step_rows(t + halo_x - (CONV_WIDTH - 1) + k, q), :]
            ct[step_rows(t, q), :] = acc
        return carry

    for t in range(tile):
        conv_step(t, 0)

    for c in range(n_slabs):
        sl = slice(c * LANES, (c + 1) * LANES)
        xc = ct[slab_rows(c), :]
        pre = jnp.dot(xc.astype(BF16), wg_ref[c], preferred_element_type=F32)
        th_r = jnp.tanh(0.5 * (pre[:, :LANES] + ba_ref[:, sl]))
        th_i = jnp.tanh(0.5 * (pre[:, LANES:] + bx_ref[:, sl]))
        neg_lam = -lam_ref[:, sl]
        softplus = jnp.maximum(neg_lam, 0.0) + jnp.log1p(jnp.exp(-jnp.abs(neg_lam)))
        log_a = (-0.5 * LRU_C * softplus) * (th_r + 1.0)
        th = jnp.tanh(log_a)
        n = -0.5 * th
        half_mult = jnp.where(n > 0.0, n * lax.rsqrt(n * (1.0 - th)), 0.0)
        a_t[slab_rows(c), :] = jnp.exp(log_a)
        b_t[slab_rows(c), :] = half_mult * (th_i + 1.0) * xc

    def scan_step(t, hs):
        out = []
        for q in range(groups):
            rows = step_rows(t, q)
            h = a_t[rows, :] * hs[q] + b_t[rows, :]
            b_t[rows, :] = h
            out.append(h)
        return tuple(out)

    h0 = tuple(hst[q * SUBLANES:(q + 1) * SUBLANES, :] for q in range(groups))
    h_last = h0
    for t in range(tile):
        h_last = scan_step(t, h_last)
    for q in range(groups):
        hst[q * SUBLANES:(q + 1) * SUBLANES, :] = h_last[q]

    for c in range(n_slabs):
        y_s[:, c * LANES:(c + 1) * LANES] = b_t[slab_rows(c), :]
    g = gr_ref[...]
    inner = g * (GELU_C0 + (GELU_C0 * GELU_C1) * (g * g))
    y = (y_s[...] * g) * (0.5 * jnp.tanh(inner) + 0.5)
    o_ref[:, 0:d_rnn] = _rms_scale(y, beta_ref[...]).astype(BF16)

    low = lax.broadcasted_iota(jnp.int32, (SUBLANES, LANES), 0) < SUBLANES // 2
    w_lo = [POOL_WINDOWS[2 * q] for q in range(groups)]
    w_hi = [POOL_WINDOWS[2 * q + 1] for q in range(groups)]
    w_vec = [jnp.where(low, float(w_lo[q]), float(w_hi[q])) for q in range(groups)]
    ramp = max(POOL_WINDOWS)

    zeros = jnp.zeros((SUBLANES, LANES), F32)
    cs = [[zeros] * groups]
    for u in range(1, tile + halo_v + 1):
        frame = [vt[step_rows(u - 1, q), :] for q in range(groups)]
        cs.append([cs[u - 1][q] + frame[q] for q in range(groups)])
        t = u - halo_v - 1
        if t < 0:
            continue
        for q in range(groups):
            win = cs[u][q] - jnp.where(low, cs[u - w_lo[q]][q], cs[u - w_hi[q]][q])
            if t < ramp:
                pos = (t_idx * tile + t + 1).astype(F32)
                mean = win / jnp.minimum(pos, w_vec[q])
            else:
                mean = win * (1.0 / w_vec[q])
            zt[step_rows(t, q), :] = mean - frame[q]

    slabs_per_group = n_slabs // len(POOL_WINDOWS)
    for g in range(len(POOL_WINDOWS)):
        z = jnp.concatenate(
            [zt[slab_rows(g * slabs_per_group + s), :] for s in range(slabs_per_group)], axis=1)
        gl = slice(g * slabs_per_group * LANES, (g + 1) * slabs_per_group * LANES)
        y_s[:, gl] = (jnp.dot(z.astype(BF16), wp_ref[g], preferred_element_type=F32)
                      + bp_ref[:, gl])
    o_ref[:, d_rnn:] = _rms_scale(y_s[...], ps_ref[...]).astype(BF16)

    xt[0:halo_x * pitch, :] = xt[tile * pitch:(tile + halo_x) * pitch, :]
    vt[0:halo_v * pitch, :] = vt[tile * pitch:(tile + halo_v) * pitch, :]


def _mixer(proj, cw, cb, wg, ba, bx, lam, beta, wp, bp, ps, casts, *, tile, d_rnn, d_pool):
    b, s, _ = proj.shape
    grid = (b, s // tile)
    cast_specs, cast_shapes = _cast_specs(casts, grid[0] * grid[1], lambda i, j: i * grid[1] + j)
    n_slabs = d_rnn // LANES
    assert d_rnn == d_pool and n_slabs % SUBLANES == 0
    assert n_slabs // len(POOL_WINDOWS) == SUBLANES // 2
    pitch = n_slabs + 1
    halo_x = SUBLANES
    halo_v = max(POOL_WINDOWS)
    assert halo_x >= CONV_WIDTH - 1 and (tile * pitch) % SUBLANES == 0
    vec = lambda width: pl.BlockSpec((1, width), lambda i, j: (0, 0))
    time_major = lambda steps: pltpu.VMEM((steps * pitch, LANES), F32)
    kern = functools.partial(_mixer_kernel, tile=tile, n_slabs=n_slabs, halo_x=halo_x,
                             halo_v=halo_v)
    return pl.pallas_call(
        kern,
        grid=grid,
        in_specs=[
            pl.BlockSpec((None, tile, d_rnn), lambda i, j: (i, j, 0)),
            pl.BlockSpec((None, tile, d_rnn), lambda i, j: (i, j, 1)),
            pl.BlockSpec((None, tile, d_pool), lambda i, j: (i, j, 2)),
            pl.BlockSpec((CONV_WIDTH, n_slabs, LANES), lambda i, j: (0, 0, 0)),
            pl.BlockSpec((n_slabs, LANES), lambda i, j: (0, 0)),
            pl.BlockSpec(wg.shape, lambda i, j: (0, 0, 0)),
            vec(d_rnn), vec(d_rnn), vec(d_rnn), vec(d_rnn),
            pl.BlockSpec(wp.shape, lambda i, j: (0, 0, 0)),
            vec(d_pool), vec(d_pool),
        ] + cast_specs,
        out_specs=[pl.BlockSpec((None, tile, d_rnn + d_pool), lambda i, j: (i, j, 0))]
        + cast_specs,
        out_shape=[jax.ShapeDtypeStruct((b, s, d_rnn + d_pool), BF16)] + cast_shapes,
        scratch_shapes=[
            time_major(tile + halo_x),
            time_major(tile),
            time_major(tile),
            time_major(tile),
            pltpu.VMEM((n_slabs, LANES), F32),
            time_major(tile + halo_v),
            time_major(tile),
            pltpu.VMEM((tile, d_rnn), F32),
        ],
        compiler_params=_params(2),
        name="mixer",
    )(proj, proj, proj, cw.reshape(CONV_WIDTH, n_slabs, LANES), cb.reshape(n_slabs, LANES),
      wg, ba, bx, lam, beta, wp, bp, ps, *casts)


def _matmul_res_kernel(a_ref, w_ref, r_ref, g_ref, o_ref, hb_ref, ss_ref):
    k = pl.program_id(2)

    def accumulate(base_ref):
        ss = []
        for cs in _col_chunks(o_ref.shape[1]):
            h = base_ref[:, cs] + jnp.dot(a_ref[...], w_ref[:, cs], preferred_element_type=F32)
            o_ref[:, cs] = h
            hb_ref[:, cs] = (h * g_ref[:, cs]).astype(BF16)
            ss.append(_lane_partial_sumsq(h))
        ss_ref[...] = _tree_sum(ss)

    @pl.when(k == 0)
    def _():
        accumulate(r_ref)

    @pl.when(k > 0)
    def _():
        accumulate(o_ref)


def _matmul_res(a, w, res, g, *, bm, bn, bk, name):
    m, k = a.shape
    n = w.shape[1]
    tile = lambda i, j, kk: (i, j)
    return pl.pallas_call(
        _matmul_res_kernel,
        grid=(m // bm, n // bn, k // bk),
        in_specs=[
            pl.BlockSpec((bm, bk), lambda i, j, kk: (i, kk)),
            pl.BlockSpec((bk, bn), lambda i, j, kk: (kk, j)),
            pl.BlockSpec((bm, bn), tile),
            pl.BlockSpec((1, bn), lambda i, j, kk: (0, j)),
        ],
        out_specs=[pl.BlockSpec((bm, bn), tile), pl.BlockSpec((bm, bn), tile),
                   pl.BlockSpec((bm, LANES), tile)],
        out_shape=[jax.ShapeDtypeStruct((m, n), F32), jax.ShapeDtypeStruct((m, n), BF16),
                   jax.ShapeDtypeStruct((m, (n // bn) * LANES), F32)],
        compiler_params=_params(3),
        name=name,
    )(a, w, res, g)


def _mlp_up_kernel(hb_ref, ss_ref, w_ref, w1_ref, w2_ref, o_ref, w1b_ref, w2b_ref, rs_ref):
    @pl.when(pl.program_id(1) == 0)
    def _():
        _row_scale(ss_ref, rs_ref, hb_ref.shape[1])

    _cast_blocks((w1_ref, w2_ref), (w1b_ref, w2b_ref))
    rs = rs_ref[...]
    for cs in _col_chunks(o_ref.shape[1]):
        z = jnp.dot(hb_ref[...], w_ref[:, cs], preferred_element_type=F32)
        o_ref[:, cs] = jnp.square(jnp.maximum(_scale_rows(z, rs), 0.0)).astype(BF16)


def _mlp_up(hb, ss, w, casts, *, bm, bn):
    m, d = hb.shape
    n = w.shape[1]
    grid = (m // bm, n // bn)
    cast_specs, cast_shapes = _cast_specs(casts, grid[0] * grid[1], lambda i, j: i * grid[1] + j)
    return pl.pallas_call(
        _mlp_up_kernel,
        grid=grid,
        in_specs=[
            pl.BlockSpec((bm, d), lambda i, j: (i, 0)),
            pl.BlockSpec((bm, ss.shape[1]), lambda i, j: (i, 0)),
            pl.BlockSpec((d, bn), lambda i, j: (0, j)),
        ] + cast_specs,
        out_specs=[pl.BlockSpec((bm, bn), lambda i, j: (i, j))] + cast_specs,
        out_shape=[jax.ShapeDtypeStruct((m, n), BF16)] + cast_shapes,
        scratch_shapes=[pltpu.VMEM((bm, LANES), F32)],
        compiler_params=_params(2),
        name="mlp_up",
    )(hb, ss, w, *casts)


def _ple_kernel(hb_ref, ssi_ref, h_ref, wg_ref, p_ref, wp_ref, gf_ref, o_ref,
                rs_ref, rsf_ref, ssacc_ref, slab_ref, *, blocks):
    i = pl.program_id(0)
    j = pl.program_id(1)
    d = hb_ref.shape[1]

    @pl.when(jnp.logical_and(i == 0, j == 0))
    def _():
        ssacc_ref[...] = jnp.zeros(ssacc_ref.shape, F32)

    @pl.when(j == 0)
    def _():
        _row_scale(ssi_ref, rs_ref, d)
        _row_scale(ssacc_ref, rsf_ref, d)
        ssacc_ref[...] = jnp.zeros(ssacc_ref.shape, F32)

    def emit_previous():
        o_ref[...] = _scale_rows(slab_ref[j], rsf_ref[...]) * gf_ref[...]

    def compute():
        rs = rs_ref[...]
        pb = p_ref[...].astype(BF16)
        ss = []
        for cs in _col_chunks(o_ref.shape[1]):
            z = _scale_rows(jnp.dot(hb_ref[...], wg_ref[:, cs], preferred_element_type=F32), rs)
            gate = 0.5 * jnp.tanh(0.5 * z) + 0.5
            emb = jnp.dot(pb, wp_ref[:, cs], preferred_element_type=F32)
            h = h_ref[:, cs] + gate * emb
            slab_ref[j, :, cs] = h
            ss.append(_lane_partial_sumsq(h))
        ssacc_ref[...] += _tree_sum(ss)

    @pl.when(i == 0)
    def _():
        compute()

    @pl.when(jnp.logical_and(i > 0, i < blocks))
    def _():
        emit_previous()
        compute()

    @pl.when(i == blocks)
    def _():
        emit_previous()


def _ple(hb, ss, h, wg, p, wp, gf, *, bm, bn):
    m, d = hb.shape
    n = wg.shape[1]
    e = p.shape[1]
    blocks = m // bm
    tiles = n // bn
    cur = lambda i: jnp.minimum(i, blocks - 1)
    col = lambda i, j: jnp.where(i < blocks, j, tiles - 1)
    prev_tile = lambda i, j: (jnp.maximum(i - 1, 0), jnp.where(i > 0, j, 0))
    return pl.pallas_call(
        functools.partial(_ple_kernel, blocks=blocks),
        grid=(blocks + 1, tiles),
        in_specs=[
            pl.BlockSpec((bm, d), lambda i, j: (cur(i), 0)),
            pl.BlockSpec((bm, ss.shape[1]), lambda i, j: (cur(i), 0)),
            pl.BlockSpec((bm, bn), lambda i, j: (cur(i), col(i, j))),
            pl.BlockSpec((d, bn), lambda i, j: (0, col(i, j))),
            pl.BlockSpec((bm, e), lambda i, j: (cur(i), 0)),
            pl.BlockSpec((e, bn), lambda i, j: (0, col(i, j))),
            pl.BlockSpec((1, bn), lambda i, j: (0, j)),
        ],
        out_specs=pl.BlockSpec((bm, bn), prev_tile),
        out_shape=jax.ShapeDtypeStruct((m, n), F32),
        scratch_shapes=[pltpu.VMEM((bm, LANES), F32), pltpu.VMEM((bm, LANES), F32),
                        pltpu.VMEM((bm, LANES), F32), pltpu.VMEM((n // bn, bm, bn), F32)],
        compiler_params=_params(2),
        name="ple_final_norm",
    )(hb, ss, h, wg, p, wp, gf)


def kernel(x, p, norm_mix_g, w_in, conv_w, conv_b, w_rg_a, b_rg_a, w_rg_x, b_rg_x, lru_lambda,
           beta_rnn, w_pool, b_pool, pool_scale, w_out, norm_mlp_g, w_up, w_down, norm_ple_g,
           w_ple_gate, w_ple_proj, final_norm_g):
    bsz, seq, d_model = x.shape
    assert w_in.shape[0] == 1, "single trunk layer"
    d_rnn = conv_w.shape[-1]
    d_pool = b_pool.shape[-1]
    m = bsz * seq
    row = lambda v: v.reshape(1, -1)
    bf16 = lambda w: w[0].astype(BF16)

    bm, bn = ROW_BLOCK, COL_TILE
    x2 = x.reshape(m, d_model)
    proj, w_out_b = _norm_proj(x2, row(norm_mix_g), bf16(w_in), (w_out[0],),
                               bm=bm, bn=bn, rc=PREP_ROWS)
    w_gates = jnp.concatenate([w_rg_a[0], w_rg_x[0]], axis=-1).astype(BF16)
    mix, w_up_b = _mixer(
        proj.reshape(bsz, seq, -1), conv_w[0], conv_b[0], w_gates, row(b_rg_a), row(b_rg_x),
        row(lru_lambda), row(beta_rnn), bf16(w_pool), row(b_pool), row(pool_scale),
        (w_up[0],), tile=MIXER_TILE, d_rnn=d_rnn, d_pool=d_pool)
    h1, hb1, ss1 = _matmul_res(mix.reshape(m, -1), w_out_b, x2, row(norm_mlp_g),
                               bm=bm, bn=bn, bk=d_rnn + d_pool, name="out_proj")
    act, w_down_b, w_gate_b = _mlp_up(hb1, ss1, w_up_b, (w_down[0], w_ple_gate[0]), bm=bm, bn=bn)
    h2, hb2, ss2 = _matmul_res(act, w_down_b, h1, row(norm_ple_g),
                               bm=bm, bn=bn, bk=DOWN_K_TILE, name="mlp_down")
    out = _ple(hb2, ss2, h2, w_gate_b, p.reshape(m, -1), bf16(w_ple_proj), row(final_norm_g),
               bm=bm, bn=PLE_COL_TILE)
    return out.reshape(bsz, seq, d_model)
```

```python
import functools

import jax
import jax.numpy as jnp
from jax import lax
from jax.experimental import pallas as pl
from jax.experimental.pallas import tpu as pltpu

F32 = jnp.float32
BF16 = jnp.bfloat16

EPS = 1e-6
LRU_C = 8.0
GELU_C0 = 0.7978845608028654
GELU_C1 = 0.044715
CONV_WIDTH = 4
POOL_WINDOWS = (2, 4, 8, 16)
LANES = 128
SUBLANES = 8
BF16_ROWS = 16
VMEM_LIMIT = 60 * 1024 * 1024
COL_CHUNK = 512

ROW_BLOCK = 1024
COL_TILE = 1024
DOWN_K_TILE = 4096
PLE_COL_TILE = 512
PREP_ROWS = 256
MIXER_TILE = 256


def _rms_scale(x, g):
    ms = jnp.mean(x * x, axis=-1, keepdims=True)
    return x * lax.rsqrt(ms + EPS) * g


def _params(n_axes):
    return pltpu.CompilerParams(
        dimension_semantics=("arbitrary",) * n_axes, vmem_limit_bytes=VMEM_LIMIT)


def _lane_tiles(width):
    return [slice(c * LANES, (c + 1) * LANES) for c in range(width // LANES)]


def _col_chunks(width):
    step = min(width, COL_CHUNK)
    return [slice(c, c + step) for c in range(0, width, step)]


def _tree_sum(parts):
    while len(parts) > 1:
        parts = [a + b for a, b in zip(parts[0::2], parts[1::2])] + parts[len(parts) & ~1:]
    return parts[0]


def _lane_partial_sumsq(h):
    return _tree_sum([jnp.square(h[:, cs]) for cs in _lane_tiles(h.shape[1])])


def _row_scale(ss_ref, rs_ref, width):
    ms = jnp.sum(ss_ref[...], axis=-1, keepdims=True) * (1.0 / width)
    rs_ref[...] = jnp.broadcast_to(lax.rsqrt(ms + EPS), rs_ref.shape)


def _scale_rows(z, rs):
    return jnp.concatenate([z[:, cs] * rs for cs in _lane_tiles(z.shape[1])], axis=1)


def _cast_specs(ws, steps, step_of):
    step_block = lambda i, j: (step_of(i, j), 0)
    specs, shapes = [], []
    for w in ws:
        rows = w.shape[0] // steps
        assert rows * steps == w.shape[0] and rows % BF16_ROWS == 0
        specs.append(pl.BlockSpec((rows, w.shape[1]), step_block))
        shapes.append(jax.ShapeDtypeStruct(w.shape, BF16))
    return specs, shapes


def _cast_blocks(srcs, dsts):
    for src, dst in zip(srcs, dsts):
        dst[...] = src[...].astype(BF16)


def _norm_proj_kernel(x_ref, g_ref, w_ref, w1_ref, o_ref, w1b_ref, xb0, ss0, xb1, ss1, *, chunks):
    i = pl.program_id(0)
    j = pl.program_id(1)
    rc, d = x_ref.shape
    odd = lax.rem(i, 2) == 1
    r0 = pl.multiple_of(jnp.minimum(j, chunks - 1) * rc, rc)

    def prepare(xb_ref, ss_ref):
        _cast_blocks((w1_ref,), (w1b_ref,))
        for rg in range(0, rc, BF16_ROWS):
            rows = pl.ds(r0 + rg, BF16_ROWS)
            sq = []
            for cs in _lane_tiles(d):
                x = x_ref[rg:rg + BF16_ROWS, cs]
                xb_ref[rows, cs] = (x * g_ref[:, cs]).astype(BF16)
                sq.append(jnp.square(x))
            ss_ref[rows, :] = _tree_sum(sq)

    def step(xb_w, ss_w, xb_r, ss_r):
        prepare(xb_w, ss_w)
        ms = jnp.sum(ss_r[...], axis=-1, keepdims=True) * (1.0 / d)
        rs = jnp.broadcast_to(lax.rsqrt(ms + EPS), ss_r.shape)
        for cs in _col_chunks(o_ref.shape[1]):
            z = jnp.dot(xb_r[...], w_ref[:, cs], preferred_element_type=F32)
            o_ref[:, cs] = _scale_rows(z, rs)

    @pl.when(i == 0)
    def _():
        prepare(xb0, ss0)

    @pl.when(jnp.logical_and(i > 0, odd))
    def _():
        step(xb1, ss1, xb0, ss0)

    @pl.when(jnp.logical_and(i > 0, jnp.logical_not(odd)))
    def _():
        step(xb0, ss0, xb1, ss1)


def _norm_proj(x, g, w, casts, *, bm, bn, rc):
    m, d = x.shape
    n = w.shape[1]
    blocks = m // bm
    chunks = bm // rc
    assert chunks <= n // bn
    first = lambda i, j: jnp.where(i > 0, j, 0)
    cast_specs, cast_shapes = _cast_specs(
        casts, blocks * chunks,
        lambda i, j: jnp.where(i > 0, (i - 1) * chunks + jnp.minimum(j, chunks - 1), 0))
    return pl.pallas_call(
        functools.partial(_norm_proj_kernel, chunks=chunks),
        grid=(blocks + 1, n // bn),
        in_specs=[
            pl.BlockSpec((rc, d), lambda i, j: (jnp.minimum(i, blocks - 1) * chunks
                                                + jnp.minimum(j, chunks - 1), 0)),
            pl.BlockSpec((1, d), lambda i, j: (0, 0)),
            pl.BlockSpec((d, bn), lambda i, j: (0, first(i, j))),
        ] + cast_specs,
        out_specs=[pl.BlockSpec((bm, bn), lambda i, j: (jnp.maximum(i - 1, 0), first(i, j)))]
        + cast_specs,
        out_shape=[jax.ShapeDtypeStruct((m, n), F32)] + cast_shapes,
        scratch_shapes=[pltpu.VMEM((bm, d), BF16), pltpu.VMEM((bm, LANES), F32)] * 2,
        compiler_params=_params(2),
        name="norm_in_proj",
    )(x, g, w, *casts)


def _mixer_kernel(proj_ref, cw_ref, cb_ref, wg_ref, ba_ref, bx_ref, lam_ref,
                  beta_ref, wp_ref, bp_ref, ps_ref, w1_ref, o_ref, w1b_ref,
                  xt, ct, a_t, b_t, hst, vt, zt, y_s, *, tile, n_slabs, halo_x, halo_v):
    t_idx = pl.program_id(1)
    _cast_blocks((w1_ref,), (w1b_ref,))
    d_rnn = n_slabs * LANES
    xr_ref = proj_ref.at[:, 0:d_rnn]
    gr_ref = proj_ref.at[:, d_rnn:2 * d_rnn]
    v_ref = proj_ref.at[:, 2 * d_rnn:3 * d_rnn]
    pitch = n_slabs + 1
    groups = n_slabs // SUBLANES

    def slab_rows(c, t0=0):
        return pl.ds(t0 * pitch + c, tile, stride=pitch)

    def step_rows(step, q):
        return pl.ds(step * pitch + q * SUBLANES, SUBLANES)

    @pl.when(t_idx == 0)
    def _():
        xt[0:halo_x * pitch, :] = jnp.zeros((halo_x * pitch, LANES), F32)
        vt[0:halo_v * pitch, :] = jnp.zeros((halo_v * pitch, LANES), F32)
        hst[...] = jnp.zeros(hst.shape, F32)

    for c in range(n_slabs):
        sl = slice(c * LANES, (c + 1) * LANES)
        xt[slab_rows(c, halo_x), :] = xr_ref[:, sl]
        vt[slab_rows(c, halo_v), :] = v_ref[:, sl]

    cw = [[cw_ref[k, q * SUBLANES:(q + 1) * SUBLANES, :] for q in range(groups)]
          for k in range(CONV_WIDTH)]
    cb = [cb_ref[q * SUBLANES:(q + 1) * SUBLANES, :] for q in range(groups)]

    def conv_step(t, carry):
        for q in range(groups):
            acc = cb[q]
            for k in range(CONV_WIDTH):
                acc = acc + cw[k][q] * xt[step_rows(t + halo_x - (CONV_WIDTH - 1) + k, q), :]
            ct[step_rows(t, q), :] = acc
        return carry

    for t in range(tile):
        conv_step(t, 0)

    for c in range(n_slabs):
        sl = slice(c * LANES, (c + 1) * LANES)
        xc = ct[slab_rows(c), :]
        pre = jnp.dot(xc.astype(BF16), wg_ref[c], preferred_element_type=F32)
        th_r = jnp.tanh(pre[:, :LANES] + ba_ref[:, sl])
        th_i = jnp.tanh(pre[:, LANES:] + bx_ref[:, sl])
        neg_lam = -lam_ref[:, sl]
        softplus = jnp.maximum(neg_lam, 0.0) + jnp.log1p(jnp.exp(-jnp.abs(neg_lam)))
        log_a = (-0.5 * LRU_C * softplus) * (th_r + 1.0)
        th = jnp.tanh(log_a)
        n = -0.5 * th
        half_mult = jnp.where(n > 0.0, n * lax.rsqrt(n * (1.0 - th)), 0.0)
        a_t[slab_rows(c), :] = jnp.exp(log_a)
        b_t[slab_rows(c), :] = half_mult * (th_i + 1.0) * xc

    def scan_step(t, hs):
        out = []
        for q in range(groups):
            rows = step_rows(t, q)
            h = a_t[rows, :] * hs[q] + b_t[rows, :]
            b_t[rows, :] = h
            out.append(h)
        return tuple(out)

    h0 = tuple(hst[q * SUBLANES:(q + 1) * SUBLANES, :] for q in range(groups))
    h_last = h0
    for t in range(tile):
        h_last = scan_step(t, h_last)
    for q in range(groups):
        hst[q * SUBLANES:(q + 1) * SUBLANES, :] = h_last[q]

    for c in range(n_slabs):
        y_s[:, c * LANES:(c + 1) * LANES] = b_t[slab_rows(c), :]
    g = gr_ref[...]
    inner = g * (GELU_C0 + (GELU_C0 * GELU_C1) * (g * g))
    y = (y_s[...] * g) * (0.5 * jnp.tanh(inner) + 0.5)
    o_ref[:, 0:d_rnn] = _rms_scale(y, beta_ref[...]).astype(BF16)

    low = lax.broadcasted_iota(jnp.int32, (SUBLANES, LANES), 0) < SUBLANES // 2
    w_lo = [POOL_WINDOWS[2 * q] for q in range(groups)]
    w_hi = [POOL_WINDOWS[2 * q + 1] for q in range(groups)]
    w_vec = [jnp.where(low, float(w_lo[q]), float(w_hi[q])) for q in range(groups)]
    ramp = max(POOL_WINDOWS)

    zeros = jnp.zeros((SUBLANES, LANES), F32)
    cs = [[zeros] * groups]
    for u in range(1, tile + halo_v + 1):
        frame = [vt[step_rows(u - 1, q), :] for q in range(groups)]
        cs.append([cs[u - 1][q] + frame[q] for q in range(groups)])
        t = u - halo_v - 1
        if t < 0:
            continue
        for q in range(groups):
            win = cs[u][q] - jnp.where(low, cs[u - w_lo[q]][q], cs[u - w_hi[q]][q])
            if t < ramp:
                pos = (t_idx * tile + t + 1).astype(F32)
                mean = win / jnp.minimum(pos, w_vec[q])
            else:
                mean = win * (1.0 / w_vec[q])
            zt[step_rows(t, q), :] = mean - frame[q]

    slabs_per_group = n_slabs // len(POOL_WINDOWS)
    for g in range(len(POOL_WINDOWS)):
        z = jnp.concatenate(
            [zt[slab_rows(g * slabs_per_group + s), :] for s in range(slabs_per_group)], axis=1)
        gl = slice(g * slabs_per_group * LANES, (g + 1) * slabs_per_group * LANES)
        y_s[:, gl] = (jnp.dot(z.astype(BF16), wp_ref[g], preferred_element_type=F32)
                      + bp_ref[:, gl])
    o_ref[:, d_rnn:] = _rms_scale(y_s[...], ps_ref[...]).astype(BF16)

    xt[0:halo_x * pitch, :] = xt[tile * pitch:(tile + halo_x) * pitch, :]
    vt[0:halo_v * pitch, :] = vt[tile * pitch:(tile + halo_v) * pitch, :]


def _mixer(proj, cw, cb, wg, ba, bx, lam, beta, wp, bp, ps, casts, *, tile, d_rnn, d_pool):
    b, s, _ = proj.shape
    grid = (b, s // tile)
    cast_specs, cast_shapes = _cast_specs(casts, grid[0] * grid[1], lambda i, j: i * grid[1] + j)
    n_slabs = d_rnn // LANES
    assert d_rnn == d_pool and n_slabs % SUBLANES == 0
    assert n_slabs // len(POOL_WINDOWS) == SUBLANES // 2
    pitch = n_slabs + 1
    halo_x = SUBLANES
    halo_v = max(POOL_WINDOWS)
    assert halo_x >= CONV_WIDTH - 1 and (tile * pitch) % SUBLANES == 0
    vec = lambda width: pl.BlockSpec((1, width), lambda i, j: (0, 0))
    time_major = lambda steps: pltpu.VMEM((steps * pitch, LANES), F32)
    kern = functools.partial(_mixer_kernel, tile=tile, n_slabs=n_slabs, halo_x=halo_x,
                             halo_v=halo_v)
    return pl.pallas_call(
        kern,
        grid=grid,
        in_specs=[
            pl.BlockSpec((None, tile, 2 * d_rnn + d_pool), lambda i, j: (i, j, 0)),
            pl.BlockSpec((CONV_WIDTH, n_slabs, LANES), lambda i, j: (0, 0, 0)),
            pl.BlockSpec((n_slabs, LANES), lambda i, j: (0, 0)),
            pl.BlockSpec(wg.shape, lambda i, j: (0, 0, 0)),
            vec(d_rnn), vec(d_rnn), vec(d_rnn), vec(d_rnn),
            pl.BlockSpec(wp.shape, lambda i, j: (0, 0, 0)),
            vec(d_pool), vec(d_pool),
        ] + cast_specs,
        out_specs=[pl.BlockSpec((None, tile, d_rnn + d_pool), lambda i, j: (i, j, 0))]
        + cast_specs,
        out_shape=[jax.ShapeDtypeStruct((b, s, d_rnn + d_pool), BF16)] + cast_shapes,
        scratch_shapes=[
            time_major(tile + halo_x),
            time_major(tile),
            time_major(tile),
            time_major(tile),
            pltpu.VMEM((n_slabs, LANES), F32),
            time_major(tile + halo_v),
            time_major(tile),
            pltpu.VMEM((tile, d_rnn), F32),
        ],
        compiler_params=_params(2),
        name="mixer",
    )(proj, cw.reshape(CONV_WIDTH, n_slabs, LANES), cb.reshape(n_slabs, LANES),
      wg, ba, bx, lam, beta, wp, bp, ps, *casts)


def _matmul_res_kernel(a_ref, w_ref, r_ref, g_ref, o_ref, hb_ref, ss_ref):
    k = pl.program_id(2)

    def accumulate(base_ref):
        ss = []
        for cs in _col_chunks(o_ref.shape[1]):
            h = base_ref[:, cs] + jnp.dot(a_ref[...], w_ref[:, cs], preferred_element_type=F32)
            o_ref[:, cs] = h
            hb_ref[:, cs] = (h * g_ref[:, cs]).astype(BF16)
            ss.append(_lane_partial_sumsq(h))
        ss_ref[...] = _tree_sum(ss)

    @pl.when(k == 0)
    def _():
        accumulate(r_ref)

    @pl.when(k > 0)
    def _():
        accumulate(o_ref)


def _matmul_res(a, w, res, g, *, bm, bn, bk, name):
    m, k = a.shape
    n = w.shape[1]
    tile = lambda i, j, kk: (i, j)
    return pl.pallas_call(
        _matmul_res_kernel,
        grid=(m // bm, n // bn, k // bk),
        in_specs=[
            pl.BlockSpec((bm, bk), lambda i, j, kk: (i, kk)),
            pl.BlockSpec((bk, bn), lambda i, j, kk: (kk, j)),
            pl.BlockSpec((bm, bn), tile),
            pl.BlockSpec((1, bn), lambda i, j, kk: (0, j)),
        ],
        out_specs=[pl.BlockSpec((bm, bn), tile), pl.BlockSpec((bm, bn), tile),
                   pl.BlockSpec((bm, LANES), tile)],
        out_shape=[jax.ShapeDtypeStruct((m, n), F32), jax.ShapeDtypeStruct((m, n), BF16),
                   jax.ShapeDtypeStruct((m, (n // bn) * LANES), F32)],
        compiler_params=_params(3),
        name=name,
    )(a, w, res, g)


def _mlp_up_kernel(hb_ref, ss_ref, w_ref, w1_ref, w2_ref, o_ref, w1b_ref, w2b_ref, rs_ref):
    @pl.when(pl.program_id(1) == 0)
    def _():
        _row_scale(ss_ref, rs_ref, hb_ref.shape[1])

    _cast_blocks((w1_ref, w2_ref), (w1b_ref, w2b_ref))
    rs = rs_ref[...]
    for cs in _col_chunks(o_ref.shape[1]):
        z = jnp.dot(hb_ref[...], w_ref[:, cs], preferred_element_type=F32)
        o_ref[:, cs] = jnp.square(jnp.maximum(_scale_rows(z, rs), 0.0)).astype(BF16)


def _mlp_up(hb, ss, w, casts, *, bm, bn):
    m, d = hb.shape
    n = w.shape[1]
    grid = (m // bm, n // bn)
    cast_specs, cast_shapes = _cast_specs(casts, grid[0] * grid[1], lambda i, j: i * grid[1] + j)
    return pl.pallas_call(
        _mlp_up_kernel,
        grid=grid,
        in_specs=[
            pl.BlockSpec((bm, d), lambda i, j: (i, 0)),
            pl.BlockSpec((bm, ss.shape[1]), lambda i, j: (i, 0)),
            pl.BlockSpec((d, bn), lambda i, j: (0, j)),
        ] + cast_specs,
        out_specs=[pl.BlockSpec((bm, bn), lambda i, j: (i, j))] + cast_specs,
        out_shape=[jax.ShapeDtypeStruct((m, n), BF16)] + cast_shapes,
        scratch_shapes=[pltpu.VMEM((bm, LANES), F32)],
        compiler_params=_params(2),
        name="mlp_up",
    )(hb, ss, w, *casts)


def _ple_kernel(hb_ref, ssi_ref, h_ref, wg_ref, p_ref, wp_ref, gf_ref, o_ref,
                rs_ref, rsf_ref, ssacc_ref, slab_ref, *, blocks):
    i = pl.program_id(0)
    j = pl.program_id(1)
    d = hb_ref.shape[1]

    @pl.when(jnp.logical_and(i == 0, j == 0))
    def _():
        ssacc_ref[...] = jnp.zeros(ssacc_ref.shape, F32)

    @pl.when(j == 0)
    def _():
        _row_scale(ssi_ref, rs_ref, d)
        _row_scale(ssacc_ref, rsf_ref, d)
        ssacc_ref[...] = jnp.zeros(ssacc_ref.shape, F32)

    def emit_previous():
        o_ref[...] = _scale_rows(slab_ref[j], rsf_ref[...]) * gf_ref[...]

    def compute():
        rs = rs_ref[...]
        pb = p_ref[...].astype(BF16)
        ss = []
        for cs in _col_chunks(o_ref.shape[1]):
            z = _scale_rows(jnp.dot(hb_ref[...], wg_ref[:, cs], preferred_element_type=F32), rs)
            gate = 0.5 * jnp.tanh(0.5 * z) + 0.5
            emb = jnp.dot(pb, wp_ref[:, cs], preferred_element_type=F32)
            h = h_ref[:, cs] + gate * emb
            slab_ref[j, :, cs] = h
            ss.append(_lane_partial_sumsq(h))
        ssacc_ref[...] += _tree_sum(ss)

    @pl.when(i == 0)
    def _():
        compute()

    @pl.when(jnp.logical_and(i > 0, i < blocks))
    def _():
        emit_previous()
        compute()

    @pl.when(i == blocks)
    def _():
        emit_previous()


def _ple(hb, ss, h, wg, p, wp, gf, *, bm, bn):
    m, d = hb.shape
    n = wg.shape[1]
    e = p.shape[1]
    blocks = m // bm
    tiles = n // bn
    cur = lambda i: jnp.minimum(i, blocks - 1)
    col = lambda i, j: jnp.where(i < blocks, j, tiles - 1)
    prev_tile = lambda i, j: (jnp.maximum(i - 1, 0), jnp.where(i > 0, j, 0))
    return pl.pallas_call(
        functools.partial(_ple_kernel, blocks=blocks),
        grid=(blocks + 1, tiles),
        in_specs=[
            pl.BlockSpec((bm, d), lambda i, j: (cur(i), 0)),
            pl.BlockSpec((bm, ss.shape[1]), lambda i, j: (cur(i), 0)),
            pl.BlockSpec((bm, bn), lambda i, j: (cur(i), col(i, j))),
            pl.BlockSpec((d, bn), lambda i, j: (0, col(i, j))),
            pl.BlockSpec((bm, e), lambda i, j: (cur(i), 0)),
            pl.BlockSpec((e, bn), lambda i, j: (0, col(i, j))),
            pl.BlockSpec((1, bn), lambda i, j: (0, j)),
        ],
        out_specs=pl.BlockSpec((bm, bn), prev_tile),
        out_shape=jax.ShapeDtypeStruct((m, n), F32),
        scratch_shapes=[pltpu.VMEM((bm, LANES), F32), pltpu.VMEM((bm, LANES), F32),
                        pltpu.VMEM((bm, LANES), F32), pltpu.VMEM((n // bn, bm, bn), F32)],
        compiler_params=_params(2),
        name="ple_final_norm",
    )(hb, ss, h, wg, p, wp, gf)


def kernel(x, p, norm_mix_g, w_in, conv_w, conv_b, w_rg_a, b_rg_a, w_rg_x, b_rg_x, lru_lambda,
           beta_rnn, w_pool, b_pool, pool_scale, w_out, norm_mlp_g, w_up, w_down, norm_ple_g,
           w_ple_gate, w_ple_proj, final_norm_g):
    bsz, seq, d_model = x.shape
    assert w_in.shape[0] == 1, "single trunk layer"
    d_rnn = conv_w.shape[-1]
    d_pool = b_pool.shape[-1]
    m = bsz * seq
    row = lambda v: v.reshape(1, -1)
    bf16 = lambda w: w[0].astype(BF16)

    bm, bn = ROW_BLOCK, COL_TILE
    x2 = x.reshape(m, d_model)
    proj, w_out_b = _norm_proj(x2, row(norm_mix_g), bf16(w_in), (w_out[0],),
                               bm=bm, bn=bn, rc=PREP_ROWS)
    w_gates = (0.5 * jnp.concatenate([w_rg_a[0], w_rg_x[0]], axis=-1)).astype(BF16)
    mix, w_up_b = _mixer(
        proj.reshape(bsz, seq, -1), conv_w[0], conv_b[0], w_gates, row(0.5 * b_rg_a),
        row(0.5 * b_rg_x),
        row(lru_lambda), row(beta_rnn), bf16(w_pool), row(b_pool), row(pool_scale),
        (w_up[0],), tile=MIXER_TILE, d_rnn=d_rnn, d_pool=d_pool)
    h1, hb1, ss1 = _matmul_res(mix.reshape(m, -1), w_out_b, x2, row(norm_mlp_g),
                               bm=bm, bn=bn, bk=d_rnn + d_pool, name="out_proj")
    act, w_down_b, w_gate_b = _mlp_up(hb1, ss1, w_up_b, (w_down[0], w_ple_gate[0]), bm=bm, bn=bn)
    h2, hb2, ss2 = _matmul_res(act, w_down_b, h1, row(norm_ple_g),
                               bm=bm, bn=bn, bk=DOWN_K_TILE, name="mlp_down")
    out = _ple(hb2, ss2, h2, w_gate_b, p.reshape(m, -1), bf16(w_ple_proj), row(final_norm_g),
               bm=bm, bn=PLE_COL_TILE)
    return out.reshape(bsz, seq, d_model)
```

```python
import functools

import jax
import jax.numpy as jnp
from jax import lax
from jax.experimental import pallas as pl
from jax.experimental.pallas import tpu as pltpu

F32 = jnp.float32
BF16 = jnp.bfloat16

EPS = 1e-6
LRU_C = 8.0
GELU_C0 = 0.7978845608028654
GELU_C1 = 0.044715
CONV_WIDTH = 4
POOL_WINDOWS = (2, 4, 8, 16)
LANES = 128
SUBLANES = 8
BF16_ROWS = 16
VMEM_LIMIT = 60 * 1024 * 1024
COL_CHUNK = 512

ROW_BLOCK = 1024
COL_TILE = 1024
DOWN_K_TILE = 4096
PLE_COL_TILE = 512
PREP_ROWS = 256
MIXER_TILE = 256


def _rms_scale(x, g):
    ms = jnp.mean(x * x, axis=-1, keepdims=True)
    return x * lax.rsqrt(ms + EPS) * g


def _params(n_axes):
    return pltpu.CompilerParams(
        dimension_semantics=("arbitrary",) * n_axes, vmem_limit_bytes=VMEM_LIMIT)


def _lane_tiles(width):
    return [slice(c * LANES, (c + 1) * LANES) for c in range(width // LANES)]


def _col_chunks(width):
    step = min(width, COL_CHUNK)
    return [slice(c, c + step) for c in range(0, width, step)]


def _tree_sum(parts):
    while len(parts) > 1:
        parts = [a + b for a, b in zip(parts[0::2], parts[1::2])] + parts[len(parts) & ~1:]
    return parts[0]


def _lane_partial_sumsq(h):
    return _tree_sum([jnp.square(h[:, cs]) for cs in _lane_tiles(h.shape[1])])


def _row_scale(ss_ref, rs_ref, width):
    ms = jnp.sum(ss_ref[...], axis=-1, keepdims=True) * (1.0 / width)
    rs_ref[...] = jnp.broadcast_to(lax.rsqrt(ms + EPS), rs_ref.shape)


def _scale_rows(z, rs):
    return jnp.concatenate([z[:, cs] * rs for cs in _lane_tiles(z.shape[1])], axis=1)


def _cast_specs(ws, steps, step_of):
    step_block = lambda i, j: (step_of(i, j), 0)
    specs, shapes = [], []
    for w in ws:
        rows = w.shape[0] // steps
        assert rows * steps == w.shape[0] and rows % BF16_ROWS == 0
        specs.append(pl.BlockSpec((rows, w.shape[1]), step_block))
        shapes.append(jax.ShapeDtypeStruct(w.shape, BF16))
    return specs, shapes


def _cast_blocks(srcs, dsts):
    for src, dst in zip(srcs, dsts):
        dst[...] = src[...].astype(BF16)


def _norm_proj_kernel(x_ref, g_ref, w_ref, w1_ref, o_ref, w1b_ref, xb0, ss0, xb1, ss1, *, chunks):
    i = pl.program_id(0)
    j = pl.program_id(1)
    rc, d = x_ref.shape
    odd = lax.rem(i, 2) == 1
    r0 = pl.multiple_of(jnp.minimum(j, chunks - 1) * rc, rc)

    def prepare(xb_ref, ss_ref):
        _cast_blocks((w1_ref,), (w1b_ref,))
        for rg in range(0, rc, BF16_ROWS):
            rows = pl.ds(r0 + rg, BF16_ROWS)
            sq = []
            for cs in _lane_tiles(d):
                x = x_ref[rg:rg + BF16_ROWS, cs]
                xb_ref[rows, cs] = (x * g_ref[:, cs]).astype(BF16)
                sq.append(jnp.square(x))
            ss_ref[rows, :] = _tree_sum(sq)

    def step(xb_w, ss_w, xb_r, ss_r):
        prepare(xb_w, ss_w)
        ms = jnp.sum(ss_r[...], axis=-1, keepdims=True) * (1.0 / d)
        rs = jnp.broadcast_to(lax.rsqrt(ms + EPS), ss_r.shape)
        for cs in _col_chunks(o_ref.shape[1]):
            z = jnp.dot(xb_r[...], w_ref[:, cs], preferred_element_type=F32)
            o_ref[:, cs] = _scale_rows(z, rs)

    @pl.when(i == 0)
    def _():
        prepare(xb0, ss0)

    @pl.when(jnp.logical_and(i > 0, odd))
    def _():
        step(xb1, ss1, xb0, ss0)

    @pl.when(jnp.logical_and(i > 0, jnp.logical_not(odd)))
    def _():
        step(xb0, ss0, xb1, ss1)


def _norm_proj(x, g, w, casts, *, bm, bn, rc):
    m, d = x.shape
    n = w.shape[1]
    blocks = m // bm
    chunks = bm // rc
    assert chunks <= n // bn
    first = lambda i, j: jnp.where(i > 0, j, 0)
    cast_specs, cast_shapes = _cast_specs(
        casts, blocks * chunks,
        lambda i, j: jnp.where(i > 0, (i - 1) * chunks + jnp.minimum(j, chunks - 1), 0))
    return pl.pallas_call(
        functools.partial(_norm_proj_kernel, chunks=chunks),
        grid=(blocks + 1, n // bn),
        in_specs=[
            pl.BlockSpec((rc, d), lambda i, j: (jnp.minimum(i, blocks - 1) * chunks
                                                + jnp.minimum(j, chunks - 1), 0)),
            pl.BlockSpec((1, d), lambda i, j: (0, 0)),
            pl.BlockSpec((d, bn), lambda i, j: (0, first(i, j))),
        ] + cast_specs,
        out_specs=[pl.BlockSpec((bm, bn), lambda i, j: (jnp.maximum(i - 1, 0), first(i, j)))]
        + cast_specs,
        out_shape=[jax.ShapeDtypeStruct((m, n), F32)] + cast_shapes,
        scratch_shapes=[pltpu.VMEM((bm, d), BF16), pltpu.VMEM((bm, LANES), F32)] * 2,
        compiler_params=_params(2),
        name="norm_in_proj",
    )(x, g, w, *casts)


def _mixer_kernel(xr_ref, gr_ref, v_ref, cw_ref, cb_ref, wg_ref, ba_ref, bx_ref, lam_ref,
                  beta_ref, wp_ref, bp_ref, ps_ref, w1_ref, o_ref, w1b_ref,
                  xt, ct, a_t, b_t, hst, vt, zt, y_s, *, tile, n_slabs, halo_x, halo_v):
    t_idx = pl.program_id(1)
    _cast_blocks((w1_ref,), (w1b_ref,))
    d_rnn = n_slabs * LANES
    pitch = n_slabs + 1
    groups = n_slabs // SUBLANES

    def slab_rows(c, t0=0):
        return pl.ds(t0 * pitch + c, tile, stride=pitch)

    def step_rows(step, q):
        return pl.ds(step * pitch + q * SUBLANES, SUBLANES)

    @pl.when(t_idx == 0)
    def _():
        xt[0:halo_x * pitch, :] = jnp.zeros((halo_x * pitch, LANES), F32)
        vt[0:halo_v * pitch, :] = jnp.zeros((halo_v * pitch, LANES), F32)
        hst[...] = jnp.zeros(hst.shape, F32)

    for c in range(n_slabs):
        sl = slice(c * LANES, (c + 1) * LANES)
        xt[slab_rows(c, halo_x), :] = xr_ref[:, sl]
        vt[slab_rows(c, halo_v), :] = v_ref[:, sl]

    cw = [[cw_ref[k, q * SUBLANES:(q + 1) * SUBLANES, :] for q in range(groups)]
          for k in range(CONV_WIDTH)]
    cb = [cb_ref[q * SUBLANES:(q + 1) * SUBLANES, :] for q in range(groups)]

    def conv_step(t, carry):
        for q in range(groups):
            acc = cb[q]
            for k in range(CONV_WIDTH):
                acc = acc + cw[k][q] * xt[step_rows(t + halo_x - (CONV_WIDTH - 1) + k, q), :]
            ct[step_rows(t, q), :] = acc
        return carry

    for t in range(tile):
        conv_step(t, 0)

    for c in range(n_slabs):
        sl = slice(c * LANES, (c + 1) * LANES)
        xc = ct[slab_rows(c), :]
        pre = jnp.dot(xc.astype(BF16), wg_ref[c], preferred_element_type=F32)
        th_r = jnp.tanh(0.5 * (pre[:, :LANES] + ba_ref[:, sl]))
        th_i = jnp.tanh(0.5 * (pre[:, LANES:] + bx_ref[:, sl]))
        neg_lam = -lam_ref[:, sl]
        softplus = jnp.maximum(neg_lam, 0.0) + jnp.log1p(jnp.exp(-jnp.abs(neg_lam)))
        log_a = (-0.5 * LRU_C * softplus) * (th_r + 1.0)
        th = jnp.tanh(log_a)
        n = -0.5 * th
        half_mult = jnp.where(n > 0.0, n * lax.rsqrt(n * (1.0 - th)), 0.0)
        a_t[slab_rows(c), :] = jnp.exp(log_a)
        b_t[slab_rows(c), :] = half_mult * (th_i + 1.0) * xc

    def scan_step(t, hs):
        out = []
        for q in range(groups):
            rows = step_rows(t, q)
            h = a_t[rows, :] * hs[q] + b_t[rows, :]
            b_t[rows, :] = h
            out.append(h)
        return tuple(out)

    h0 = tuple(hst[q * SUBLANES:(q + 1) * SUBLANES, :] for q in range(groups))
    h_last = h0
    for t in range(tile):
        h_last = scan_step(t, h_last)
    for q in range(groups):
        hst[q * SUBLANES:(q + 1) * SUBLANES, :] = h_last[q]

    for c in range(n_slabs):
        y_s[:, c * LANES:(c + 1) * LANES] = b_t[slab_rows(c), :]
    g = gr_ref[...]
    inner = g * (GELU_C0 + (GELU_C0 * GELU_C1) * (g * g))
    y = (y_s[...] * g) * (0.5 * jnp.tanh(inner) + 0.5)
    o_ref[:, 0:d_rnn] = _rms_scale(y, beta_ref[...]).astype(BF16)

    low = lax.broadcasted_iota(jnp.int32, (SUBLANES, LANES), 0) < SUBLANES // 2
    w_lo = [POOL_WINDOWS[2 * q] for q in range(groups)]
    w_hi = [POOL_WINDOWS[2 * q + 1] for q in range(groups)]
    w_vec = [jnp.where(low, float(w_lo[q]), float(w_hi[q])) for q in range(groups)]
    ramp = max(POOL_WINDOWS)

    zeros = jnp.zeros((SUBLANES, LANES), F32)
    cs = [[zeros] * groups]
    for u in range(1, tile + halo_v + 1):
        frame = [vt[step_rows(u - 1, q), :] for q in range(groups)]
        cs.append([cs[u - 1][q] + frame[q] for q in range(groups)])
        t = u - halo_v - 1
        if t < 0:
            continue
        for q in range(groups):
            win = cs[u][q] - jnp.where(low, cs[u - w_lo[q]][q], cs[u - w_hi[q]][q])
            if t < ramp:
                pos = (t_idx * tile + t + 1).astype(F32)
                mean = win / jnp.minimum(pos, w_vec[q])
            else:
                mean = win * (1.0 / w_vec[q])
            zt[step_rows(t, q), :] = mean - frame[q]

    slabs_per_group = n_slabs // len(POOL_WINDOWS)
    for g in range(len(POOL_WINDOWS)):
        z = jnp.concatenate(
            [zt[slab_rows(g * slabs_per_group + s), :] for s in range(slabs_per_group)], axis=1)
        gl = slice(g * slabs_per_group * LANES, (g + 1) * slabs_per_group * LANES)
        y_s[:, gl] = (jnp.dot(z.astype(BF16), wp_ref[g], preferred_element_type=F32)
                      + bp_ref[:, gl])
    o_ref[:, d_rnn:] = _rms_scale(y_s[...], ps_ref[...]).astype(BF16)

    xt[0:halo_x * pitch, :] = xt[tile * pitch:(tile + halo_x) * pitch, :]
    vt[0:halo_v * pitch, :] = vt[tile * pitch:(tile + halo_v) * pitch, :]


def _mixer(proj, cw, cb, wg, ba, bx, lam, beta, wp, bp, ps, casts, *, tile, d_rnn, d_pool):
    b, s, _ = proj.shape
    grid = (b, s // tile)
    cast_specs, cast_shapes = _cast_specs(casts, grid[0] * grid[1], lambda i, j: i * grid[1] + j)
    n_slabs = d_rnn // LANES
    assert d_rnn == d_pool and n_slabs % SUBLANES == 0
    assert n_slabs // len(POOL_WINDOWS) == SUBLANES // 2
    pitch = n_slabs + 1
    halo_x = SUBLANES
    halo_v = max(POOL_WINDOWS)
    assert halo_x >= CONV_WIDTH - 1 and (tile * pitch) % SUBLANES == 0
    vec = lambda width: pl.BlockSpec((1, width), lambda i, j: (0, 0))
    time_major = lambda steps: pltpu.VMEM((steps * pitch, LANES), F32)
    kern = functools.partial(_mixer_kernel, tile=tile, n_slabs=n_slabs, halo_x=halo_x,
                             halo_v=halo_v)
    return pl.pallas_call(
        kern,
        grid=grid,
        in_specs=[
            pl.BlockSpec((None, tile, d_rnn), lambda i, j: (i, j, 0)),
            pl.BlockSpec((None, tile, d_rnn), lambda i, j: (i, j, 1)),
            pl.BlockSpec((None, tile, d_pool), lambda i, j: (i, j, 2)),
            pl.BlockSpec((CONV_WIDTH, n_slabs, LANES), lambda i, j: (0, 0, 0)),
            pl.BlockSpec((n_slabs, LANES), lambda i, j: (0, 0)),
            pl.BlockSpec(wg.shape, lambda i, j: (0, 0, 0)),
            vec(d_rnn), vec(d_rnn), vec(d_rnn), vec(d_rnn),
            pl.BlockSpec(wp.shape, lambda i, j: (0, 0, 0)),
            vec(d_pool), vec(d_pool),
        ] + cast_specs,
        out_specs=[pl.BlockSpec((None, tile, d_rnn + d_pool), lambda i, j: (i, j, 0))]
        + cast_specs,
        out_shape=[jax.ShapeDtypeStruct((b, s, d_rnn + d_pool), BF16)] + cast_shapes,
        scratch_shapes=[
            time_major(tile + halo_x),
            time_major(tile),
            time_major(tile),
            time_major(tile),
            pltpu.VMEM((n_slabs, LANES), F32),
            time_major(tile + halo_v),
            time_major(tile),
            pltpu.VMEM((tile, d_rnn), F32),
        ],
        compiler_params=_params(2),
        name="mixer",
    )(proj, proj, proj, cw.reshape(CONV_WIDTH, n_slabs, LANES), cb.reshape(n_slabs, LANES),
      wg, ba, bx, lam, beta, wp, bp, ps, *casts)


def _matmul_res_kernel(a_ref, w_ref, r_ref, g_ref, o_ref, hb_ref, ss_ref):
    k = pl.program_id(2)

    def accumulate(base_ref):
        ss = []
        for cs in _col_chunks(o_ref.shape[1]):
            h = base_ref[:, cs] + jnp.dot(a_ref[...], w_ref[:, cs], preferred_element_type=F32)
            o_ref[:, cs] = h
            hb_ref[:, cs] = (h * g_ref[:, cs]).astype(BF16)
            ss.append(_lane_partial_sumsq(h))
        ss_ref[...] = _tree_sum(ss)

    @pl.when(k == 0)
    def _():
        accumulate(r_ref)

    @pl.when(k > 0)
    def _():
        accumulate(o_ref)


def _matmul_res(a, w, res, g, *, bm, bn, bk, name):
    m, k = a.shape
    n = w.shape[1]
    tile = lambda i, j, kk: (i, j)
    return pl.pallas_call(
        _matmul_res_kernel,
        grid=(m // bm, n // bn, k // bk),
        in_specs=[
            pl.BlockSpec((bm, bk), lambda i, j, kk: (i, kk)),
            pl.BlockSpec((bk, bn), lambda i, j, kk: (kk, j)),
            pl.BlockSpec((bm, bn), tile),
            pl.BlockSpec((1, bn), lambda i, j, kk: (0, j)),
        ],
        out_specs=[pl.BlockSpec((bm, bn), tile), pl.BlockSpec((bm, bn), tile),
                   pl.BlockSpec((bm, LANES), tile)],
        out_shape=[jax.ShapeDtypeStruct((m, n), F32), jax.ShapeDtypeStruct((m, n), BF16),
                   jax.ShapeDtypeStruct((m, (n // bn) * LANES), F32)],
        compiler_params=_params(3),
        name=name,
    )(a, w, res, g)


def _mlp_up_kernel(hb_ref, ss_ref, w_ref, w1_ref, w2_ref, o_ref, w1b_ref, w2b_ref, rs_ref):
    @pl.when(pl.program_id(1) == 0)
    def _():
        _row_scale(ss_ref, rs_ref, hb_ref.shape[1])

    _cast_blocks((w1_ref, w2_ref), (w1b_ref, w2b_ref))
    rs = rs_ref[...]
    for cs in _col_chunks(o_ref.shape[1]):
        z = jnp.dot(hb_ref[...], w_ref[:, cs], preferred_element_type=F32)
        o_ref[:, cs] = jnp.square(jnp.maximum(_scale_rows(z, rs), 0.0)).astype(BF16)


def _mlp_up(hb, ss, w, casts, *, bm, bn):
    m, d = hb.shape
    n = w.shape[1]
    grid = (m // bm, n // bn)
    cast_specs, cast_shapes = _cast_specs(casts, grid[0] * grid[1], lambda i, j: i * grid[1] + j)
    return pl.pallas_call(
        _mlp_up_kernel,
        grid=grid,
        in_specs=[
            pl.BlockSpec((bm, d), lambda i, j: (i, 0)),
            pl.BlockSpec((bm, ss.shape[1]), lambda i, j: (i, 0)),
            pl.BlockSpec((d, bn), lambda i, j: (0, j)),
        ] + cast_specs,
        out_specs=[pl.BlockSpec((bm, bn), lambda i, j: (i, j))] + cast_specs,
        out_shape=[jax.ShapeDtypeStruct((m, n), BF16)] + cast_shapes,
        scratch_shapes=[pltpu.VMEM((bm, LANES), F32)],
        compiler_params=_params(2),
        name="mlp_up",
    )(hb, ss, w, *casts)


def _ple_kernel(hb_ref, ssi_ref, h_ref, wg_ref, p_ref, wp_ref, gf_ref, o_ref,
                rs_ref, rsf_ref, ssacc_ref, slab_ref, *, blocks):
    i = pl.program_id(0)
    j = pl.program_id(1)
    d = hb_ref.shape[1]

    @pl.when(jnp.logical_and(i == 0, j == 0))
    def _():
        ssacc_ref[...] = jnp.zeros(ssacc_ref.shape, F32)

    @pl.when(j == 0)
    def _():
        _row_scale(ssi_ref, rs_ref, d)
        _row_scale(ssacc_ref, rsf_ref, d)
        ssacc_ref[...] = jnp.zeros(ssacc_ref.shape, F32)

    def emit_previous():
        o_ref[...] = _scale_rows(slab_ref[j], rsf_ref[...]) * gf_ref[...]

    def compute():
        rs = rs_ref[...]
        pb = p_ref[...].astype(BF16)
        ss = []
        for cs in _col_chunks(o_ref.shape[1]):
            z = _scale_rows(jnp.dot(hb_ref[...], wg_ref[:, cs], preferred_element_type=F32), rs)
            gate = 0.5 * jnp.tanh(0.5 * z) + 0.5
            emb = jnp.dot(pb, wp_ref[:, cs], preferred_element_type=F32)
            h = h_ref[:, cs] + gate * emb
            slab_ref[j, :, cs] = h
            ss.append(_lane_partial_sumsq(h))
        ssacc_ref[...] += _tree_sum(ss)

    @pl.when(i == 0)
    def _():
        compute()

    @pl.when(jnp.logical_and(i > 0, i < blocks))
    def _():
        emit_previous()
        compute()

    @pl.when(i == blocks)
    def _():
        emit_previous()


def _ple(hb, ss, h, wg, p, wp, gf, *, bm, bn):
    m, d = hb.shape
    n = wg.shape[1]
    e = p.shape[1]
    blocks = m // bm
    tiles = n // bn
    cur = lambda i: jnp.minimum(i, blocks - 1)
    col = lambda i, j: jnp.where(i < blocks, j, tiles - 1)
    prev_tile = lambda i, j: (jnp.maximum(i - 1, 0), jnp.where(i > 0, j, 0))
    return pl.pallas_call(
        functools.partial(_ple_kernel, blocks=blocks),
        grid=(blocks + 1, tiles),
        in_specs=[
            pl.BlockSpec((bm, d), lambda i, j: (cur(i), 0)),
            pl.BlockSpec((bm, ss.shape[1]), lambda i, j: (cur(i), 0)),
            pl.BlockSpec((bm, bn), lambda i, j: (cur(i), col(i, j))),
            pl.BlockSpec((d, bn), lambda i, j: (0, col(i, j))),
            pl.BlockSpec((bm, e), lambda i, j: (cur(i), 0)),
            pl.BlockSpec((e, bn), lambda i, j: (0, col(i, j))),
            pl.BlockSpec((1, bn), lambda i, j: (0, j)),
        ],
        out_specs=pl.BlockSpec((bm, bn), prev_tile),
        out_shape=jax.ShapeDtypeStruct((m, n), F32),
        scratch_shapes=[pltpu.VMEM((bm, LANES), F32), pltpu.VMEM((bm, LANES), F32),
                        pltpu.VMEM((bm, LANES), F32), pltpu.VMEM((n // bn, bm, bn), F32)],
        compiler_params=_params(2),
        name="ple_final_norm",
    )(hb, ss, h, wg, p, wp, gf)


def kernel(x, p, norm_mix_g, w_in, conv_w, conv_b, w_rg_a, b_rg_a, w_rg_x, b_rg_x, lru_lambda,
           beta_rnn, w_pool, b_pool, pool_scale, w_out, norm_mlp_g, w_up, w_down, norm_ple_g,
           w_ple_gate, w_ple_proj, final_norm_g):
    bsz, seq, d_model = x.shape
    assert w_in.shape[0] == 1, "single trunk layer"
    d_rnn = conv_w.shape[-1]
    d_pool = b_pool.shape[-1]
    m = bsz * seq
    row = lambda v: v.reshape(1, -1)
    bf16 = lambda w: w[0].astype(BF16)

    bm, bn = ROW_BLOCK, COL_TILE
    x2 = x.reshape(m, d_model)
    proj, w_out_b = _norm_proj(x2, row(norm_mix_g), bf16(w_in), (w_out[0],),
                               bm=bm, bn=bn, rc=PREP_ROWS)
    w_gates = jnp.concatenate([w_rg_a[0], w_rg_x[0]], axis=-1).astype(BF16)
    mix, w_up_b = _mixer(
        proj.reshape(bsz, seq, -1), conv_w[0], conv_b[0], w_gates, row(b_rg_a), row(b_rg_x),
        row(lru_lambda), row(beta_rnn), bf16(w_pool), row(b_pool), row(pool_scale),
        (w_up[0],), tile=MIXER_TILE, d_rnn=d_rnn, d_pool=d_pool)
    h1, hb1, ss1 = _matmul_res(mix.reshape(m, -1), w_out_b, x2, row(norm_mlp_g),
                               bm=bm, bn=bn, bk=d_rnn + d_pool, name="out_proj")
    act, w_down_b, w_gate_b = _mlp_up(hb1, ss1, w_up_b, (w_down[0], w_ple_gate[0]), bm=bm, bn=bn)
    h2, hb2, ss2 = _matmul_res(act, w_down_b, h1, row(norm_ple_g),
                               bm=bm, bn=bn, bk=DOWN_K_TILE, name="mlp_down")
    out = _ple(hb2, ss2, h2, w_gate_b, p.reshape(m, -1), bf16(w_ple_proj), row(final_norm_g),
               bm=bm, bn=PLE_COL_TILE)
    return out.reshape(bsz, seq, d_model)
```

```python
import functools

import jax
import jax.numpy as jnp
from jax import lax
from jax.experimental import pallas as pl
from jax.experimental.pallas import tpu as pltpu

F32 = jnp.float32
BF16 = jnp.bfloat16

EPS = 1e-6
LRU_C = 8.0
GELU_C0 = 0.7978845608028654
GELU_C1 = 0.044715
CONV_WIDTH = 4
POOL_WINDOWS = (2, 4, 8, 16)
LANES = 128
SUBLANES = 8
BF16_ROWS = 16
VMEM_LIMIT = 60 * 1024 * 1024
COL_CHUNK = 512

ROW_BLOCK = 1024
COL_TILE = 1024
DOWN_K_TILE = 4096
PLE_COL_TILE = 512
PREP_ROWS = 256
MIXER_TILE = 256


def _rms_scale(x, g):
    ms = jnp.mean(x * x, axis=-1, keepdims=True)
    return x * lax.rsqrt(ms + EPS) * g


def _params(n_axes):
    return pltpu.CompilerParams(
        dimension_semantics=("arbitrary",) * n_axes, vmem_limit_bytes=VMEM_LIMIT)


def _lane_tiles(width):
    return [slice(c * LANES, (c + 1) * LANES) for c in range(width // LANES)]


def _col_chunks(width, chunk=COL_CHUNK):
    step = min(width, chunk)
    return [slice(c, c + step) for c in range(0, width, step)]


def _tree_sum(parts):
    while len(parts) > 1:
        parts = [a + b for a, b in zip(parts[0::2], parts[1::2])] + parts[len(parts) & ~1:]
    return parts[0]


def _lane_partial_sumsq(h):
    return _tree_sum([jnp.square(h[:, cs]) for cs in _lane_tiles(h.shape[1])])


def _row_scale(ss_ref, rs_ref, width):
    ms = jnp.sum(ss_ref[...], axis=-1, keepdims=True) * (1.0 / width)
    rs_ref[...] = jnp.broadcast_to(lax.rsqrt(ms + EPS), rs_ref.shape)


def _scale_rows(z, rs):
    return jnp.concatenate([z[:, cs] * rs for cs in _lane_tiles(z.shape[1])], axis=1)


def _cast_specs(ws, steps, step_of):
    step_block = lambda i, j: (step_of(i, j), 0)
    specs, shapes = [], []
    for w in ws:
        rows = w.shape[0] // steps
        assert rows * steps == w.shape[0] and rows % BF16_ROWS == 0
        specs.append(pl.BlockSpec((rows, w.shape[1]), step_block))
        shapes.append(jax.ShapeDtypeStruct(w.shape, BF16))
    return specs, shapes


def _cast_blocks(srcs, dsts):
    for src, dst in zip(srcs, dsts):
        dst[...] = src[...].astype(BF16)


def _norm_proj_kernel(x_ref, g_ref, w_ref, w1_ref, o_ref, w1b_ref, xb0, ss0, xb1, ss1, *, chunks):
    i = pl.program_id(0)
    j = pl.program_id(1)
    rc, d = x_ref.shape
    odd = lax.rem(i, 2) == 1
    r0 = pl.multiple_of(jnp.minimum(j, chunks - 1) * rc, rc)

    def prepare(xb_ref, ss_ref, part=0, parts=1):
        if part == 0:
            _cast_blocks((w1_ref,), (w1b_ref,))
        span = rc // parts
        for rg in range(part * span, (part + 1) * span, BF16_ROWS):
            rows = pl.ds(r0 + rg, BF16_ROWS)
            acc = [jnp.zeros((BF16_ROWS, LANES), F32)] * 2
            for c, cs in enumerate(_lane_tiles(d)):
                x = x_ref[rg:rg + BF16_ROWS, cs]
                xb_ref[rows, cs] = (x * g_ref[:, cs]).astype(BF16)
                acc[c % 2] = acc[c % 2] + jnp.square(x)
            ss_ref[rows, :] = acc[0] + acc[1]

    def step(xb_w, ss_w, xb_r, ss_r):
        ms = jnp.sum(ss_r[...], axis=-1, keepdims=True) * (1.0 / d)
        rs = jnp.broadcast_to(lax.rsqrt(ms + EPS), ss_r.shape)
        col_chunks = _col_chunks(o_ref.shape[1], 256)
        for part, cs in enumerate(col_chunks):
            z = jnp.dot(xb_r[...], w_ref[:, cs], preferred_element_type=F32)
            prepare(xb_w, ss_w, part, len(col_chunks))
            o_ref[:, cs] = _scale_rows(z, rs)

    @pl.when(i == 0)
    def _():
        prepare(xb0, ss0)

    @pl.when(jnp.logical_and(i > 0, odd))
    def _():
        step(xb1, ss1, xb0, ss0)

    @pl.when(jnp.logical_and(i > 0, jnp.logical_not(odd)))
    def _():
        step(xb0, ss0, xb1, ss1)


def _norm_proj(x, g, w, casts, *, bm, bn, rc):
    m, d = x.shape
    n = w.shape[1]
    blocks = m // bm
    chunks = bm // rc
    assert chunks <= n // bn
    first = lambda i, j: jnp.where(i > 0, j, 0)
    cast_specs, cast_shapes = _cast_specs(
        casts, blocks * chunks,
        lambda i, j: jnp.where(i > 0, (i - 1) * chunks + jnp.minimum(j, chunks - 1), 0))
    return pl.pallas_call(
        functools.partial(_norm_proj_kernel, chunks=chunks),
        grid=(blocks + 1, n // bn),
        in_specs=[
            pl.BlockSpec((rc, d), lambda i, j: (jnp.minimum(i, blocks - 1) * chunks
                                                + jnp.minimum(j, chunks - 1), 0)),
            pl.BlockSpec((1, d), lambda i, j: (0, 0)),
            pl.BlockSpec((d, bn), lambda i, j: (0, first(i, j))),
        ] + cast_specs,
        out_specs=[pl.BlockSpec((bm, bn), lambda i, j: (jnp.maximum(i - 1, 0), first(i, j)))]
        + cast_specs,
        out_shape=[jax.ShapeDtypeStruct((m, n), F32)] + cast_shapes,
        scratch_shapes=[pltpu.VMEM((bm, d), BF16), pltpu.VMEM((bm, LANES), F32)] * 2,
        compiler_params=_params(2),
        name="norm_in_proj",
    )(x, g, w, *casts)


def _mixer_kernel(xr_ref, gr_ref, v_ref, cw_ref, cb_ref, wg_ref, ba_ref, bx_ref, lam_ref,
                  beta_ref, wp_ref, bp_ref, ps_ref, w1_ref, o_ref, w1b_ref,
                  xt, ct, a_t, b_t, hst, vt, zt, y_s, *, tile, n_slabs, halo_x, halo_v):
    t_idx = pl.program_id(1)
    _cast_blocks((w1_ref,), (w1b_ref,))
    d_rnn = n_slabs * LANES
    pitch = n_slabs + 1
    groups = n_slabs // SUBLANES

    def slab_rows(c, t0=0):
        return pl.ds(t0 * pitch + c, tile, stride=pitch)

    def step_rows(step, q):
        return pl.ds(step * pitch + q * SUBLANES, SUBLANES)

    @pl.when(t_idx == 0)
    def _():
        xt[0:halo_x * pitch, :] = jnp.zeros((halo_x * pitch, LANES), F32)
        vt[0:halo_v * pitch, :] = jnp.zeros((halo_v * pitch, LANES), F32)
        hst[...] = jnp.zeros(hst.shape, F32)

    for c in range(n_slabs):
        sl = slice(c * LANES, (c + 1) * LANES)
        xt[slab_rows(c, halo_x), :] = xr_ref[:, sl]
        vt[slab_rows(c, halo_v), :] = v_ref[:, sl]

    cw = [[cw_ref[k, q * SUBLANES:(q + 1) * SUBLANES, :] for q in range(groups)]
          for k in range(CONV_WIDTH)]
    cb = [cb_ref[q * SUBLANES:(q + 1) * SUBLANES, :] for q in range(groups)]

    def conv_step(t, carry):
        for q in range(groups):
            acc = cb[q]
            for k in range(CONV_WIDTH):
                acc = acc + cw[k][q] * xt[step_rows(t + halo_x - (CONV_WIDTH - 1) + k, q), :]
            ct[step_rows(t, q), :] = acc
        return carry

    for t in range(tile):
        conv_step(t, 0)

    for c in range(n_slabs):
        sl = slice(c * LANES, (c + 1) * LANES)
        xc = ct[slab_rows(c), :]
        pre = jnp.dot(xc.astype(BF16), wg_ref[c], preferred_element_type=F32)
        th_r = jnp.tanh(0.5 * (pre[:, :LANES] + ba_ref[:, sl]))
        th_i = jnp.tanh(0.5 * (pre[:, LANES:] + bx_ref[:, sl]))
        neg_lam = -lam_ref[:, sl]
        softplus = jnp.maximum(neg_lam, 0.0) + jnp.log1p(jnp.exp(-jnp.abs(neg_lam)))
        log_a = (-0.5 * LRU_C * softplus) * (th_r + 1.0)
        th = jnp.tanh(log_a)
        n = -0.5 * th
        half_mult = jnp.where(n > 0.0, n * lax.rsqrt(n * (1.0 - th)), 0.0)
        a_t[slab_rows(c), :] = jnp.exp(log_a)
        b_t[slab_rows(c), :] = half_mult * (th_i + 1.0) * xc

    def scan_step(t, hs):
        out = []
        for q in range(groups):
            rows = step_rows(t, q)
            h = a_t[rows, :] * hs[q] + b_t[rows, :]
            b_t[rows, :] = h
            out.append(h)
        return tuple(out)

    h0 = tuple(hst[q * SUBLANES:(q + 1) * SUBLANES, :] for q in range(groups))
    h_last = h0
    for t in range(tile):
        h_last = scan_step(t, h_last)
    for q in range(groups):
        hst[q * SUBLANES:(q + 1) * SUBLANES, :] = h_last[q]

    for c in range(n_slabs):
        y_s[:, c * LANES:(c + 1) * LANES] = b_t[slab_rows(c), :]
    g = gr_ref[...]
    inner = g * (GELU_C0 + (GELU_C0 * GELU_C1) * (g * g))
    y = (y_s[...] * g) * (0.5 * jnp.tanh(inner) + 0.5)
    o_ref[:, 0:d_rnn] = _rms_scale(y, beta_ref[...]).astype(BF16)

    low = lax.broadcasted_iota(jnp.int32, (SUBLANES, LANES), 0) < SUBLANES // 2
    w_lo = [POOL_WINDOWS[2 * q] for q in range(groups)]
    w_hi = [POOL_WINDOWS[2 * q + 1] for q in range(groups)]
    w_vec = [jnp.where(low, float(w_lo[q]), float(w_hi[q])) for q in range(groups)]
    ramp = max(POOL_WINDOWS)

    zeros = jnp.zeros((SUBLANES, LANES), F32)
    cs = [[zeros] * groups]
    for u in range(1, tile + halo_v + 1):
        frame = [vt[step_rows(u - 1, q), :] for q in range(groups)]
        cs.append([cs[u - 1][q] + frame[q] for q in range(groups)])
        t = u - halo_v - 1
        if t < 0:
            continue
        for q in range(groups):
            win = cs[u][q] - jnp.where(low, cs[u - w_lo[q]][q], cs[u - w_hi[q]][q])
            if t < ramp:
                pos = (t_idx * tile + t + 1).astype(F32)
                mean = win / jnp.minimum(pos, w_vec[q])
            else:
                mean = win * (1.0 / w_vec[q])
            zt[step_rows(t, q), :] = mean - frame[q]

    slabs_per_group = n_slabs // len(POOL_WINDOWS)
    for g in range(len(POOL_WINDOWS)):
        z = jnp.concatenate(
            [zt[slab_rows(g * slabs_per_group + s), :] for s in range(slabs_per_group)], axis=1)
        gl = slice(g * slabs_per_group * LANES, (g + 1) * slabs_per_group * LANES)
        y_s[:, gl] = (jnp.dot(z.astype(BF16), wp_ref[g], preferred_element_type=F32)
                      + bp_ref[:, gl])
    o_ref[:, d_rnn:] = _rms_scale(y_s[...], ps_ref[...]).astype(BF16)

    xt[0:halo_x * pitch, :] = xt[tile * pitch:(tile + halo_x) * pitch, :]
    vt[0:halo_v * pitch, :] = vt[tile * pitch:(tile + halo_v) * pitch, :]


def _mixer(proj, cw, cb, wg, ba, bx, lam, beta, wp, bp, ps, casts, *, tile, d_rnn, d_pool):
    b, s, _ = proj.shape
    grid = (b, s // tile)
    cast_specs, cast_shapes = _cast_specs(casts, grid[0] * grid[1], lambda i, j: i * grid[1] + j)
    n_slabs = d_rnn // LANES
    assert d_rnn == d_pool and n_slabs % SUBLANES == 0
    assert n_slabs // len(POOL_WINDOWS) == SUBLANES // 2
    pitch = n_slabs + 1
    halo_x = SUBLANES
    halo_v = max(POOL_WINDOWS)
    assert halo_x >= CONV_WIDTH - 1 and (tile * pitch) % SUBLANES == 0
    vec = lambda width: pl.BlockSpec((1, width), lambda i, j: (0, 0))
    time_major = lambda steps: pltpu.VMEM((steps * pitch, LANES), F32)
    kern = functools.partial(_mixer_kernel, tile=tile, n_slabs=n_slabs, halo_x=halo_x,
                             halo_v=halo_v)
    return pl.pallas_call(
        kern,
        grid=grid,
        in_specs=[
            pl.BlockSpec((None, tile, d_rnn), lambda i, j: (i, j, 0)),
            pl.BlockSpec((None, tile, d_rnn), lambda i, j: (i, j, 1)),
            pl.BlockSpec((None, tile, d_pool), lambda i, j: (i, j, 2)),
            pl.BlockSpec((CONV_WIDTH, n_slabs, LANES), lambda i, j: (0, 0, 0)),
            pl.BlockSpec((n_slabs, LANES), lambda i, j: (0, 0)),
            pl.BlockSpec(wg.shape, lambda i, j: (0, 0, 0)),
            vec(d_rnn), vec(d_rnn), vec(d_rnn), vec(d_rnn),
            pl.BlockSpec(wp.shape, lambda i, j: (0, 0, 0)),
            vec(d_pool), vec(d_pool),
        ] + cast_specs,
        out_specs=[pl.BlockSpec((None, tile, d_rnn + d_pool), lambda i, j: (i, j, 0))]
        + cast_specs,
        out_shape=[jax.ShapeDtypeStruct((b, s, d_rnn + d_pool), BF16)] + cast_shapes,
        scratch_shapes=[
            time_major(tile + halo_x),
            time_major(tile),
            time_major(tile),
            time_major(tile),
            pltpu.VMEM((n_slabs, LANES), F32),
            time_major(tile + halo_v),
            time_major(tile),
            pltpu.VMEM((tile, d_rnn), F32),
        ],
        compiler_params=_params(2),
        name="mixer",
    )(proj, proj, proj, cw.reshape(CONV_WIDTH, n_slabs, LANES), cb.reshape(n_slabs, LANES),
      wg, ba, bx, lam, beta, wp, bp, ps, *casts)


def _matmul_res_kernel(a_ref, w_ref, r_ref, g_ref, o_ref, hb_ref, ss_ref):
    k = pl.program_id(2)

    def accumulate(base_ref):
        ss = []
        for cs in _col_chunks(o_ref.shape[1]):
            h = base_ref[:, cs] + jnp.dot(a_ref[...], w_ref[:, cs], preferred_element_type=F32)
            o_ref[:, cs] = h
            hb_ref[:, cs] = (h * g_ref[:, cs]).astype(BF16)
            ss.append(_lane_partial_sumsq(h))
        ss_ref[...] = _tree_sum(ss)

    @pl.when(k == 0)
    def _():
        accumulate(r_ref)

    @pl.when(k > 0)
    def _():
        accumulate(o_ref)


def _matmul_res(a, w, res, g, *, bm, bn, bk, name):
    m, k = a.shape
    n = w.shape[1]
    tile = lambda i, j, kk: (i, j)
    return pl.pallas_call(
        _matmul_res_kernel,
        grid=(m // bm, n // bn, k // bk),
        in_specs=[
            pl.BlockSpec((bm, bk), lambda i, j, kk: (i, kk)),
            pl.BlockSpec((bk, bn), lambda i, j, kk: (kk, j)),
            pl.BlockSpec((bm, bn), tile),
            pl.BlockSpec((1, bn), lambda i, j, kk: (0, j)),
        ],
        out_specs=[pl.BlockSpec((bm, bn), tile), pl.BlockSpec((bm, bn), tile),
                   pl.BlockSpec((bm, LANES), tile)],
        out_shape=[jax.ShapeDtypeStruct((m, n), F32), jax.ShapeDtypeStruct((m, n), BF16),
                   jax.ShapeDtypeStruct((m, (n // bn) * LANES), F32)],
        compiler_params=_params(3),
        name=name,
    )(a, w, res, g)


def _mlp_up_kernel(hb_ref, ss_ref, w_ref, w1_ref, w2_ref, o_ref, w1b_ref, w2b_ref, rs_ref):
    @pl.when(pl.program_id(1) == 0)
    def _():
        _row_scale(ss_ref, rs_ref, hb_ref.shape[1])

    _cast_blocks((w1_ref, w2_ref), (w1b_ref, w2b_ref))
    rs = rs_ref[...]
    for cs in _col_chunks(o_ref.shape[1]):
        z = jnp.dot(hb_ref[...], w_ref[:, cs], preferred_element_type=F32)
        o_ref[:, cs] = jnp.square(jnp.maximum(_scale_rows(z, rs), 0.0)).astype(BF16)


def _mlp_up(hb, ss, w, casts, *, bm, bn):
    m, d = hb.shape
    n = w.shape[1]
    grid = (m // bm, n // bn)
    cast_specs, cast_shapes = _cast_specs(casts, grid[0] * grid[1], lambda i, j: i * grid[1] + j)
    return pl.pallas_call(
        _mlp_up_kernel,
        grid=grid,
        in_specs=[
            pl.BlockSpec((bm, d), lambda i, j: (i, 0)),
            pl.BlockSpec((bm, ss.shape[1]), lambda i, j: (i, 0)),
            pl.BlockSpec((d, bn), lambda i, j: (0, j)),
        ] + cast_specs,
        out_specs=[pl.BlockSpec((bm, bn), lambda i, j: (i, j))] + cast_specs,
        out_shape=[jax.ShapeDtypeStruct((m, n), BF16)] + cast_shapes,
        scratch_shapes=[pltpu.VMEM((bm, LANES), F32)],
        compiler_params=_params(2),
        name="mlp_up",
    )(hb, ss, w, *casts)


def _ple_kernel(hb_ref, ssi_ref, h_ref, wg_ref, p_ref, wp_ref, gf_ref, o_ref,
                rs_ref, rsf_ref, ssacc_ref, slab_ref, *, blocks):
    i = pl.program_id(0)
    j = pl.program_id(1)
    d = hb_ref.shape[1]

    @pl.when(jnp.logical_and(i == 0, j == 0))
    def _():
        ssacc_ref[...] = jnp.zeros(ssacc_ref.shape, F32)

    @pl.when(j == 0)
    def _():
        _row_scale(ssi_ref, rs_ref, d)
        _row_scale(ssacc_ref, rsf_ref, d)
        ssacc_ref[...] = jnp.zeros(ssacc_ref.shape, F32)

    def emit_previous():
        o_ref[...] = _scale_rows(slab_ref[j], rsf_ref[...]) * gf_ref[...]

    def compute():
        rs = rs_ref[...]
        pb = p_ref[...].astype(BF16)
        ss = []
        for cs in _col_chunks(o_ref.shape[1]):
            z = _scale_rows(jnp.dot(hb_ref[...], wg_ref[:, cs], preferred_element_type=F32), rs)
            gate = 0.5 * jnp.tanh(0.5 * z) + 0.5
            emb = jnp.dot(pb, wp_ref[:, cs], preferred_element_type=F32)
            h = h_ref[:, cs] + gate * emb
            slab_ref[j, :, cs] = h
            ss.append(_lane_partial_sumsq(h))
        ssacc_ref[...] += _tree_sum(ss)

    @pl.when(i == 0)
    def _():
        compute()

    @pl.when(jnp.logical_and(i > 0, i < blocks))
    def _():
        emit_previous()
        compute()

    @pl.when(i == blocks)
    def _():
        emit_previous()


def _ple(hb, ss, h, wg, p, wp, gf, *, bm, bn):
    m, d = hb.shape
    n = wg.shape[1]
    e = p.shape[1]
    blocks = m // bm
    tiles = n // bn
    cur = lambda i: jnp.minimum(i, blocks - 1)
    col = lambda i, j: jnp.where(i < blocks, j, tiles - 1)
    prev_tile = lambda i, j: (jnp.maximum(i - 1, 0), jnp.where(i > 0, j, 0))
    return pl.pallas_call(
        functools.partial(_ple_kernel, blocks=blocks),
        grid=(blocks + 1, tiles),
        in_specs=[
            pl.BlockSpec((bm, d), lambda i, j: (cur(i), 0)),
            pl.BlockSpec((bm, ss.shape[1]), lambda i, j: (cur(i), 0)),
            pl.BlockSpec((bm, bn), lambda i, j: (cur(i), col(i, j))),
            pl.BlockSpec((d, bn), lambda i, j: (0, col(i, j))),
            pl.BlockSpec((bm, e), lambda i, j: (cur(i), 0)),
            pl.BlockSpec((e, bn), lambda i, j: (0, col(i, j))),
            pl.BlockSpec((1, bn), lambda i, j: (0, j)),
        ],
        out_specs=pl.BlockSpec((bm, bn), prev_tile),
        out_shape=jax.ShapeDtypeStruct((m, n), F32),
        scratch_shapes=[pltpu.VMEM((bm, LANES), F32), pltpu.VMEM((bm, LANES), F32),
                        pltpu.VMEM((bm, LANES), F32), pltpu.VMEM((n // bn, bm, bn), F32)],
        compiler_params=_params(2),
        name="ple_final_norm",
    )(hb, ss, h, wg, p, wp, gf)


def kernel(x, p, norm_mix_g, w_in, conv_w, conv_b, w_rg_a, b_rg_a, w_rg_x, b_rg_x, lru_lambda,
           beta_rnn, w_pool, b_pool, pool_scale, w_out, norm_mlp_g, w_up, w_down, norm_ple_g,
           w_ple_gate, w_ple_proj, final_norm_g):
    bsz, seq, d_model = x.shape
    assert w_in.shape[0] == 1, "single trunk layer"
    d_rnn = conv_w.shape[-1]
    d_pool = b_pool.shape[-1]
    m = bsz * seq
    row = lambda v: v.reshape(1, -1)
    bf16 = lambda w: w[0].astype(BF16)

    bm, bn = ROW_BLOCK, COL_TILE
    x2 = x.reshape(m, d_model)
    proj, w_out_b = _norm_proj(x2, row(norm_mix_g), bf16(w_in), (w_out[0],),
                               bm=bm, bn=bn, rc=PREP_ROWS)
    w_gates = jnp.concatenate([w_rg_a[0], w_rg_x[0]], axis=-1).astype(BF16)
    mix, w_up_b = _mixer(
        proj.reshape(bsz, seq, -1), conv_w[0], conv_b[0], w_gates, row(b_rg_a), row(b_rg_x),
        row(lru_lambda), row(beta_rnn), bf16(w_pool), row(b_pool), row(pool_scale),
        (w_up[0],), tile=MIXER_TILE, d_rnn=d_rnn, d_pool=d_pool)
    h1, hb1, ss1 = _matmul_res(mix.reshape(m, -1), w_out_b, x2, row(norm_mlp_g),
                               bm=bm, bn=bn, bk=d_rnn + d_pool, name="out_proj")
    act, w_down_b, w_gate_b = _mlp_up(hb1, ss1, w_up_b, (w_down[0], w_ple_gate[0]), bm=bm, bn=bn)
    h2, hb2, ss2 = _matmul_res(act, w_down_b, h1, row(norm_ple_g),
                               bm=bm, bn=bn, bk=DOWN_K_TILE, name="mlp_down")
    out = _ple(hb2, ss2, h2, w_gate_b, p.reshape(m, -1), bf16(w_ple_proj), row(final_norm_g),
               bm=bm, bn=PLE_COL_TILE)
    return out.reshape(bsz, seq, d_model)
```

```python
import functools

import jax
import jax.numpy as jnp
from jax import lax
from jax.experimental import pallas as pl
from jax.experimental.pallas import tpu as pltpu

F32 = jnp.float32
BF16 = jnp.bfloat16

EPS = 1e-6
LRU_C = 8.0
GELU_C0 = 0.7978845608028654
GELU_C1 = 0.044715
CONV_WIDTH = 4
POOL_WINDOWS = (2, 4, 8, 16)
LANES = 128
SUBLANES = 8
BF16_ROWS = 16
VMEM_LIMIT = 60 * 1024 * 1024
COL_CHUNK = 512

ROW_BLOCK = 1024
COL_TILE = 1024
UP_COL_TILE = 2048
DOWN_K_TILE = 4096
PLE_COL_TILE = 512
PREP_ROWS = 256
MIXER_TILE = 256


def _rms_scale(x, g):
    ms = jnp.mean(x * x, axis=-1, keepdims=True)
    return x * lax.rsqrt(ms + EPS) * g


def _params(n_axes):
    return pltpu.CompilerParams(
        dimension_semantics=("arbitrary",) * n_axes, vmem_limit_bytes=VMEM_LIMIT)


def _lane_tiles(width):
    return [slice(c * LANES, (c + 1) * LANES) for c in range(width // LANES)]


def _col_chunks(width):
    step = min(width, COL_CHUNK)
    return [slice(c, c + step) for c in range(0, width, step)]


def _tree_sum(parts):
    while len(parts) > 1:
        parts = [a + b for a, b in zip(parts[0::2], parts[1::2])] + parts[len(parts) & ~1:]
    return parts[0]


def _lane_partial_sumsq(h):
    return _tree_sum([jnp.square(h[:, cs]) for cs in _lane_tiles(h.shape[1])])


def _row_scale(ss_ref, rs_ref, width):
    ms = jnp.sum(ss_ref[...], axis=-1, keepdims=True) * (1.0 / width)
    rs_ref[...] = jnp.broadcast_to(lax.rsqrt(ms + EPS), rs_ref.shape)


def _scale_rows(z, rs):
    return jnp.concatenate([z[:, cs] * rs for cs in _lane_tiles(z.shape[1])], axis=1)


def _cast_specs(ws, steps, step_of):
    step_block = lambda i, j: (step_of(i, j), 0)
    specs, shapes = [], []
    for w in ws:
        rows = w.shape[0] // steps
        assert rows * steps == w.shape[0] and rows % BF16_ROWS == 0
        specs.append(pl.BlockSpec((rows, w.shape[1]), step_block))
        shapes.append(jax.ShapeDtypeStruct(w.shape, BF16))
    return specs, shapes


def _cast_blocks(srcs, dsts):
    for src, dst in zip(srcs, dsts):
        dst[...] = src[...].astype(BF16)


def _norm_proj_kernel(x_ref, g_ref, w_ref, w1_ref, o_ref, w1b_ref, xb0, ss0, xb1, ss1, *, chunks):
    i = pl.program_id(0)
    j = pl.program_id(1)
    rc, d = x_ref.shape
    odd = lax.rem(i, 2) == 1
    r0 = pl.multiple_of(jnp.minimum(j, chunks - 1) * rc, rc)

    def prepare(xb_ref, ss_ref):
        _cast_blocks((w1_ref,), (w1b_ref,))
        for rg in range(0, rc, BF16_ROWS):
            rows = pl.ds(r0 + rg, BF16_ROWS)
            sq = []
            for cs in _lane_tiles(d):
                x = x_ref[rg:rg + BF16_ROWS, cs]
                xb_ref[rows, cs] = (x * g_ref[:, cs]).astype(BF16)
                sq.append(jnp.square(x))
            ss_ref[rows, :] = _tree_sum(sq)

    def step(xb_w, ss_w, xb_r, ss_r):
        prepare(xb_w, ss_w)
        ms = jnp.sum(ss_r[...], axis=-1, keepdims=True) * (1.0 / d)
        rs = jnp.broadcast_to(lax.rsqrt(ms + EPS), ss_r.shape)
        for cs in _col_chunks(o_ref.shape[1]):
            z = jnp.dot(xb_r[...], w_ref[:, cs], preferred_element_type=F32)
            o_ref[:, cs] = _scale_rows(z, rs)

    @pl.when(i == 0)
    def _():
        prepare(xb0, ss0)

    @pl.when(jnp.logical_and(i > 0, odd))
    def _():
        step(xb1, ss1, xb0, ss0)

    @pl.when(jnp.logical_and(i > 0, jnp.logical_not(odd)))
    def _():
        step(xb0, ss0, xb1, ss1)


def _norm_proj(x, g, w, casts, *, bm, bn, rc):
    m, d = x.shape
    n = w.shape[1]
    blocks = m // bm
    chunks = bm // rc
    assert chunks <= n // bn
    first = lambda i, j: jnp.where(i > 0, j, 0)
    cast_specs, cast_shapes = _cast_specs(
        casts, blocks * chunks,
        lambda i, j: jnp.where(i > 0, (i - 1) * chunks + jnp.minimum(j, chunks - 1), 0))
    return pl.pallas_call(
        functools.partial(_norm_proj_kernel, chunks=chunks),
        grid=(blocks + 1, n // bn),
        in_specs=[
            pl.BlockSpec((rc, d), lambda i, j: (jnp.minimum(i, blocks - 1) * chunks
                                                + jnp.minimum(j, chunks - 1), 0)),
            pl.BlockSpec((1, d), lambda i, j: (0, 0)),
            pl.BlockSpec((d, bn), lambda i, j: (0, first(i, j))),
        ] + cast_specs,
        out_specs=[pl.BlockSpec((bm, bn), lambda i, j: (jnp.maximum(i - 1, 0), first(i, j)))]
        + cast_specs,
        out_shape=[jax.ShapeDtypeStruct((m, n), F32)] + cast_shapes,
        scratch_shapes=[pltpu.VMEM((bm, d), BF16), pltpu.VMEM((bm, LANES), F32)] * 2,
        compiler_params=_params(2),
        name="norm_in_proj",
    )(x, g, w, *casts)


def _mixer_kernel(xr_ref, gr_ref, v_ref, cw_ref, cb_ref, wg_ref, ba_ref, bx_ref, lam_ref,
                  beta_ref, wp_ref, bp_ref, ps_ref, w1_ref, o_ref, w1b_ref,
                  xt, ct, a_t, b_t, hst, vt, zt, y_s, *, tile, n_slabs, halo_x, halo_v):
    t_idx = pl.program_id(1)
    _cast_blocks((w1_ref,), (w1b_ref,))
    d_rnn = n_slabs * LANES
    pitch = n_slabs + 1
    groups = n_slabs // SUBLANES

    def slab_rows(c, t0=0):
        return pl.ds(t0 * pitch + c, tile, stride=pitch)

    def step_rows(step, q):
        return pl.ds(step * pitch + q * SUBLANES, SUBLANES)

    @pl.when(t_idx == 0)
    def _():
        xt[0:halo_x * pitch, :] = jnp.zeros((halo_x * pitch, LANES), F32)
        vt[0:halo_v * pitch, :] = jnp.zeros((halo_v * pitch, LANES), F32)
        hst[...] = jnp.zeros(hst.shape, F32)

    for c in range(n_slabs):
        sl = slice(c * LANES, (c + 1) * LANES)
        xt[slab_rows(c, halo_x), :] = xr_ref[:, sl]
        vt[slab_rows(c, halo_v), :] = v_ref[:, sl]

    cw = [[cw_ref[k, q * SUBLANES:(q + 1) * SUBLANES, :] for q in range(groups)]
          for k in range(CONV_WIDTH)]
    cb = [cb_ref[q * SUBLANES:(q + 1) * SUBLANES, :] for q in range(groups)]

    def conv_step(t, carry):
        for q in range(groups):
            acc = cb[q]
            for k in range(CONV_WIDTH):
                acc = acc + cw[k][q] * xt[step_rows(t + halo_x - (CONV_WIDTH - 1) + k, q), :]
            ct[step_rows(t, q), :] = acc
        return carry

    for t in range(tile):
        conv_step(t, 0)

    for c in range(n_slabs):
        sl = slice(c * LANES, (c + 1) * LANES)
        xc = ct[slab_rows(c), :]
        pre = jnp.dot(xc.astype(BF16), wg_ref[c], preferred_element_type=F32)
        th_r = jnp.tanh(0.5 * (pre[:, :LANES] + ba_ref[:, sl]))
        th_i = jnp.tanh(0.5 * (pre[:, LANES:] + bx_ref[:, sl]))
        neg_lam = -lam_ref[:, sl]
        softplus = jnp.maximum(neg_lam, 0.0) + jnp.log1p(jnp.exp(-jnp.abs(neg_lam)))
        log_a = (-0.5 * LRU_C * softplus) * (th_r + 1.0)
        th = jnp.tanh(log_a)
        n = -0.5 * th
        half_mult = jnp.where(n > 0.0, n * lax.rsqrt(n * (1.0 - th)), 0.0)
        a_t[slab_rows(c), :] = jnp.exp(log_a)
        b_t[slab_rows(c), :] = half_mult * (th_i + 1.0) * xc

    def scan_step(t, hs):
        out = []
        for q in range(groups):
            rows = step_rows(t, q)
            h = a_t[rows, :] * hs[q] + b_t[rows, :]
            b_t[rows, :] = h
            out.append(h)
        return tuple(out)

    h0 = tuple(hst[q * SUBLANES:(q + 1) * SUBLANES, :] for q in range(groups))
    h_last = h0
    for t in range(tile):
        h_last = scan_step(t, h_last)
    for q in range(groups):
        hst[q * SUBLANES:(q + 1) * SUBLANES, :] = h_last[q]

    for c in range(n_slabs):
        y_s[:, c * LANES:(c + 1) * LANES] = b_t[slab_rows(c), :]
    g = gr_ref[...]
    inner = g * (GELU_C0 + (GELU_C0 * GELU_C1) * (g * g))
    y = (y_s[...] * g) * (0.5 * jnp.tanh(inner) + 0.5)
    o_ref[:, 0:d_rnn] = _rms_scale(y, beta_ref[...]).astype(BF16)

    low = lax.broadcasted_iota(jnp.int32, (SUBLANES, LANES), 0) < SUBLANES // 2
    w_lo = [POOL_WINDOWS[2 * q] for q in range(groups)]
    w_hi = [POOL_WINDOWS[2 * q + 1] for q in range(groups)]
    w_vec = [jnp.where(low, float(w_lo[q]), float(w_hi[q])) for q in range(groups)]
    ramp = max(POOL_WINDOWS)

    zeros = jnp.zeros((SUBLANES, LANES), F32)
    cs = [[zeros] * groups]
    for u in range(1, tile + halo_v + 1):
        frame = [vt[step_rows(u - 1, q), :] for q in range(groups)]
        cs.append([cs[u - 1][q] + frame[q] for q in range(groups)])
        t = u - halo_v - 1
        if t < 0:
            continue
        for q in range(groups):
            win = cs[u][q] - jnp.where(low, cs[u - w_lo[q]][q], cs[u - w_hi[q]][q])
            if t < ramp:
                pos = (t_idx * tile + t + 1).astype(F32)
                mean = win / jnp.minimum(pos, w_vec[q])
            else:
                mean = win * (1.0 / w_vec[q])
            zt[step_rows(t, q), :] = mean - frame[q]

    slabs_per_group = n_slabs // len(POOL_WINDOWS)
    for g in range(len(POOL_WINDOWS)):
        z = jnp.concatenate(
            [zt[slab_rows(g * slabs_per_group + s), :] for s in range(slabs_per_group)], axis=1)
        gl = slice(g * slabs_per_group * LANES, (g + 1) * slabs_per_group * LANES)
        y_s[:, gl] = (jnp.dot(z.astype(BF16), wp_ref[g], preferred_element_type=F32)
                      + bp_ref[:, gl])
    o_ref[:, d_rnn:] = _rms_scale(y_s[...], ps_ref[...]).astype(BF16)

    xt[0:halo_x * pitch, :] = xt[tile * pitch:(tile + halo_x) * pitch, :]
    vt[0:halo_v * pitch, :] = vt[tile * pitch:(tile + halo_v) * pitch, :]


def _mixer(proj, cw, cb, wg, ba, bx, lam, beta, wp, bp, ps, casts, *, tile, d_rnn, d_pool):
    b, s, _ = proj.shape
    grid = (b, s // tile)
    cast_specs, cast_shapes = _cast_specs(casts, grid[0] * grid[1], lambda i, j: i * grid[1] + j)
    n_slabs = d_rnn // LANES
    assert d_rnn == d_pool and n_slabs % SUBLANES == 0
    assert n_slabs // len(POOL_WINDOWS) == SUBLANES // 2
    pitch = n_slabs + 1
    halo_x = SUBLANES
    halo_v = max(POOL_WINDOWS)
    assert halo_x >= CONV_WIDTH - 1 and (tile * pitch) % SUBLANES == 0
    vec = lambda width: pl.BlockSpec((1, width), lambda i, j: (0, 0))
    time_major = lambda steps: pltpu.VMEM((steps * pitch, LANES), F32)
    kern = functools.partial(_mixer_kernel, tile=tile, n_slabs=n_slabs, halo_x=halo_x,
                             halo_v=halo_v)
    return pl.pallas_call(
        kern,
        grid=grid,
        in_specs=[
            pl.BlockSpec((None, tile, d_rnn), lambda i, j: (i, j, 0)),
            pl.BlockSpec((None, tile, d_rnn), lambda i, j: (i, j, 1)),
            pl.BlockSpec((None, tile, d_pool), lambda i, j: (i, j, 2)),
            pl.BlockSpec((CONV_WIDTH, n_slabs, LANES), lambda i, j: (0, 0, 0)),
            pl.BlockSpec((n_slabs, LANES), lambda i, j: (0, 0)),
            pl.BlockSpec(wg.shape, lambda i, j: (0, 0, 0)),
            vec(d_rnn), vec(d_rnn), vec(d_rnn), vec(d_rnn),
            pl.BlockSpec(wp.shape, lambda i, j: (0, 0, 0)),
            vec(d_pool), vec(d_pool),
        ] + cast_specs,
        out_specs=[pl.BlockSpec((None, tile, d_rnn + d_pool), lambda i, j: (i, j, 0))]
        + cast_specs,
        out_shape=[jax.ShapeDtypeStruct((b, s, d_rnn + d_pool), BF16)] + cast_shapes,
        scratch_shapes=[
            time_major(tile + halo_x),
            time_major(tile),
            time_major(tile),
            time_major(tile),
            pltpu.VMEM((n_slabs, LANES), F32),
            time_major(tile + halo_v),
            time_major(tile),
            pltpu.VMEM((tile, d_rnn), F32),
        ],
        compiler_params=_params(2),
        name="mixer",
    )(proj, proj, proj, cw.reshape(CONV_WIDTH, n_slabs, LANES), cb.reshape(n_slabs, LANES),
      wg, ba, bx, lam, beta, wp, bp, ps, *casts)


def _matmul_res_kernel(a_ref, w_ref, r_ref, g_ref, o_ref, hb_ref, ss_ref):
    k = pl.program_id(2)

    def accumulate(base_ref):
        ss = []
        for cs in _col_chunks(o_ref.shape[1]):
            h = base_ref[:, cs] + jnp.dot(a_ref[...], w_ref[:, cs], preferred_element_type=F32)
            o_ref[:, cs] = h
            hb_ref[:, cs] = (h * g_ref[:, cs]).astype(BF16)
            ss.append(_lane_partial_sumsq(h))
        ss_ref[...] = _tree_sum(ss)

    @pl.when(k == 0)
    def _():
        accumulate(r_ref)

    @pl.when(k > 0)
    def _():
        accumulate(o_ref)


def _matmul_res(a, w, res, g, *, bm, bn, bk, name):
    m, k = a.shape
    n = w.shape[1]
    tile = lambda i, j, kk: (i, j)
    return pl.pallas_call(
        _matmul_res_kernel,
        grid=(m // bm, n // bn, k // bk),
        in_specs=[
            pl.BlockSpec((bm, bk), lambda i, j, kk: (i, kk)),
            pl.BlockSpec((bk, bn), lambda i, j, kk: (kk, j)),
            pl.BlockSpec((bm, bn), tile),
            pl.BlockSpec((1, bn), lambda i, j, kk: (0, j)),
        ],
        out_specs=[pl.BlockSpec((bm, bn), tile), pl.BlockSpec((bm, bn), tile),
                   pl.BlockSpec((bm, LANES), tile)],
        out_shape=[jax.ShapeDtypeStruct((m, n), F32), jax.ShapeDtypeStruct((m, n), BF16),
                   jax.ShapeDtypeStruct((m, (n // bn) * LANES), F32)],
        compiler_params=_params(3),
        name=name,
    )(a, w, res, g)


def _mlp_up_kernel(hb_ref, ss_ref, w_ref, w1_ref, w2_ref, o_ref, w1b_ref, w2b_ref, rs_ref):
    @pl.when(pl.program_id(1) == 0)
    def _():
        _row_scale(ss_ref, rs_ref, hb_ref.shape[1])

    _cast_blocks((w1_ref, w2_ref), (w1b_ref, w2b_ref))
    rs = rs_ref[...]
    for cs in _col_chunks(o_ref.shape[1]):
        z = jnp.dot(hb_ref[...], w_ref[:, cs], preferred_element_type=F32)
        o_ref[:, cs] = jnp.square(jnp.maximum(_scale_rows(z, rs), 0.0)).astype(BF16)


def _mlp_up(hb, ss, w, casts, *, bm, bn):
    m, d = hb.shape
    n = w.shape[1]
    grid = (m // bm, n // bn)
    cast_specs, cast_shapes = _cast_specs(casts, grid[0] * grid[1], lambda i, j: i * grid[1] + j)
    return pl.pallas_call(
        _mlp_up_kernel,
        grid=grid,
        in_specs=[
            pl.BlockSpec((bm, d), lambda i, j: (i, 0), pipeline_mode=pl.Buffered(1)),
            pl.BlockSpec((bm, ss.shape[1]), lambda i, j: (i, 0), pipeline_mode=pl.Buffered(1)),
            pl.BlockSpec((d, bn), lambda i, j: (0, j)),
        ] + cast_specs,
        out_specs=[pl.BlockSpec((bm, bn), lambda i, j: (i, j))] + cast_specs,
        out_shape=[jax.ShapeDtypeStruct((m, n), BF16)] + cast_shapes,
        scratch_shapes=[pltpu.VMEM((bm, LANES), F32)],
        compiler_params=_params(2),
        name="mlp_up",
    )(hb, ss, w, *casts)


def _ple_kernel(hb_ref, ssi_ref, h_ref, wg_ref, p_ref, wp_ref, gf_ref, o_ref,
                rs_ref, rsf_ref, ssacc_ref, slab_ref, *, blocks):
    i = pl.program_id(0)
    j = pl.program_id(1)
    d = hb_ref.shape[1]

    @pl.when(jnp.logical_and(i == 0, j == 0))
    def _():
        ssacc_ref[...] = jnp.zeros(ssacc_ref.shape, F32)

    @pl.when(j == 0)
    def _():
        _row_scale(ssi_ref, rs_ref, d)
        _row_scale(ssacc_ref, rsf_ref, d)
        ssacc_ref[...] = jnp.zeros(ssacc_ref.shape, F32)

    def emit_previous():
        o_ref[...] = _scale_rows(slab_ref[j], rsf_ref[...]) * gf_ref[...]

    def compute():
        rs = rs_ref[...]
        pb = p_ref[...].astype(BF16)
        ss = []
        for cs in _col_chunks(o_ref.shape[1]):
            z = _scale_rows(jnp.dot(hb_ref[...], wg_ref[:, cs], preferred_element_type=F32), rs)
            gate = 0.5 * jnp.tanh(0.5 * z) + 0.5
            emb = jnp.dot(pb, wp_ref[:, cs], preferred_element_type=F32)
            h = h_ref[:, cs] + gate * emb
            slab_ref[j, :, cs] = h
            ss.append(_lane_partial_sumsq(h))
        ssacc_ref[...] += _tree_sum(ss)

    @pl.when(i == 0)
    def _():
        compute()

    @pl.when(jnp.logical_and(i > 0, i < blocks))
    def _():
        emit_previous()
        compute()

    @pl.when(i == blocks)
    def _():
        emit_previous()


def _ple(hb, ss, h, wg, p, wp, gf, *, bm, bn):
    m, d = hb.shape
    n = wg.shape[1]
    e = p.shape[1]
    blocks = m // bm
    tiles = n // bn
    cur = lambda i: jnp.minimum(i, blocks - 1)
    col = lambda i, j: jnp.where(i < blocks, j, tiles - 1)
    prev_tile = lambda i, j: (jnp.maximum(i - 1, 0), jnp.where(i > 0, j, 0))
    return pl.pallas_call(
        functools.partial(_ple_kernel, blocks=blocks),
        grid=(blocks + 1, tiles),
        in_specs=[
            pl.BlockSpec((bm, d), lambda i, j: (cur(i), 0)),
            pl.BlockSpec((bm, ss.shape[1]), lambda i, j: (cur(i), 0)),
            pl.BlockSpec((bm, bn), lambda i, j: (cur(i), col(i, j))),
            pl.BlockSpec((d, bn), lambda i, j: (0, col(i, j))),
            pl.BlockSpec((bm, e), lambda i, j: (cur(i), 0)),
            pl.BlockSpec((e, bn), lambda i, j: (0, col(i, j))),
            pl.BlockSpec((1, bn), lambda i, j: (0, j)),
        ],
        out_specs=pl.BlockSpec((bm, bn), prev_tile),
        out_shape=jax.ShapeDtypeStruct((m, n), F32),
        scratch_shapes=[pltpu.VMEM((bm, LANES), F32), pltpu.VMEM((bm, LANES), F32),
                        pltpu.VMEM((bm, LANES), F32), pltpu.VMEM((n // bn, bm, bn), F32)],
        compiler_params=_params(2),
        name="ple_final_norm",
    )(hb, ss, h, wg, p, wp, gf)


def kernel(x, p, norm_mix_g, w_in, conv_w, conv_b, w_rg_a, b_rg_a, w_rg_x, b_rg_x, lru_lambda,
           beta_rnn, w_pool, b_pool, pool_scale, w_out, norm_mlp_g, w_up, w_down, norm_ple_g,
           w_ple_gate, w_ple_proj, final_norm_g):
    bsz, seq, d_model = x.shape
    assert w_in.shape[0] == 1, "single trunk layer"
    d_rnn = conv_w.shape[-1]
    d_pool = b_pool.shape[-1]
    m = bsz * seq
    row = lambda v: v.reshape(1, -1)
    bf16 = lambda w: w[0].astype(BF16)

    bm, bn = ROW_BLOCK, COL_TILE
    x2 = x.reshape(m, d_model)
    proj, w_out_b = _norm_proj(x2, row(norm_mix_g), bf16(w_in), (w_out[0],),
                               bm=bm, bn=bn, rc=PREP_ROWS)
    w_gates = jnp.concatenate([w_rg_a[0], w_rg_x[0]], axis=-1).astype(BF16)
    mix, w_up_b = _mixer(
        proj.reshape(bsz, seq, -1), conv_w[0], conv_b[0], w_gates, row(b_rg_a), row(b_rg_x),
        row(lru_lambda), row(beta_rnn), bf16(w_pool), row(b_pool), row(pool_scale),
        (w_up[0],), tile=MIXER_TILE, d_rnn=d_rnn, d_pool=d_pool)
    h1, hb1, ss1 = _matmul_res(mix.reshape(m, -1), w_out_b, x2, row(norm_mlp_g),
                               bm=bm, bn=bn, bk=d_rnn + d_pool, name="out_proj")
    act, w_down_b, w_gate_b = _mlp_up(hb1, ss1, w_up_b, (w_down[0], w_ple_gate[0]),
                                      bm=bm, bn=UP_COL_TILE)
    h2, hb2, ss2 = _matmul_res(act, w_down_b, h1, row(norm_ple_g),
                               bm=bm, bn=bn, bk=DOWN_K_TILE, name="mlp_down")
    out = _ple(hb2, ss2, h2, w_gate_b, p.reshape(m, -1), bf16(w_ple_proj), row(final_norm_g),
               bm=bm, bn=PLE_COL_TILE)
    return out.reshape(bsz, seq, d_model)
```
